```python
import math
import jax, jax.numpy as jnp
from jax import lax
import numpy as np

D_MODEL = 1024
BATCH = 4
SEQ = 4096
DEPTH = 2

GRID_W = 64
CTX_LEN = 256
EPS = 1e-6
N_BRANCH = 4
POOL_W = 256
POOL_GROUPS = 4
POOL_WINDOWS = (2, 4, 8, 16)
SSM_W = 256
SSM_CH = 16
SSM_GROUPS = SSM_W // SSM_CH
SSM_N = 64
DT_MIN = 1e-3
DT_MAX = 1e-1
GMLP_W = 256
GMLP_HEADS = 4
CHUNK = 128
NA_HEADS = 4
HEAD_DIM = 64
NA_W = NA_HEADS * HEAD_DIM
WIN_ROWS = 8
WIN_COLS = 16
ROPE_BASE = 10000.0
BRANCH_W = 256
IN_W = POOL_W + SSM_W + 2 * GMLP_W + 3 * NA_W
N_EXPERTS = 32
TOP_K = 4
D_EXPERT = 1024
SWIGLU_LIMIT = 7.0
SWIGLU_ALPHA = 1.702
MOE_BLOCK = 128

kernel_name = 'hybrid_pool_s5_gmlp_natten_moe_dit'

F32 = jnp.float32


def rmsnorm(x, g):
    xf = x.astype(F32)
    return (xf * lax.rsqrt(jnp.mean(xf * xf, axis=-1, keepdims=True) + EPS) * g.astype(F32)).astype(x.dtype)


def layernorm(x, g, b):
    xf = x.astype(F32)
    mu = jnp.mean(xf, axis=-1, keepdims=True)
    var = jnp.mean((xf - mu) ** 2, axis=-1, keepdims=True)
    return ((xf - mu) * lax.rsqrt(var + EPS) * g.astype(F32) + b.astype(F32)).astype(x.dtype)


def modulate(h, shift, scale):
    return h * (1.0 + scale) + shift


def axial_rope(x):
    L_ = x.shape[1]
    t = jnp.arange(L_)
    half = HEAD_DIM // 2
    nf = half // 2
    inv = ROPE_BASE ** (-jnp.arange(nf, dtype=F32) / nf)
    xf = x.astype(F32)

    def rot(xp, pos):
        ang = pos.astype(F32)[:, None] * inv
        cos = jnp.cos(ang)[None, :, None, :]
        sin = jnp.sin(ang)[None, :, None, :]
        x1, x2 = xp[..., :nf], xp[..., nf:]
        return jnp.concatenate([x1 * cos - x2 * sin, x2 * cos + x1 * sin], axis=-1)

    return jnp.concatenate([rot(xf[..., :half], t // GRID_W), rot(xf[..., half:], t % GRID_W)], axis=-1).astype(x.dtype)


def pool_mix(u, w_pool, pool_scale):
    B_, L_, _ = u.shape
    cg = POOL_W // POOL_GROUPS
    uf = u.reshape(B_, L_, POOL_GROUPS, cg).astype(F32)
    cs = jnp.concatenate([jnp.zeros_like(uf[:, :1]), jnp.cumsum(uf, axis=1)], axis=1)
    win = jnp.asarray(POOL_WINDOWS, jnp.int32)
    t = jnp.arange(L_, dtype=jnp.int32)[:, None]
    lo = jnp.clip(t - win // 2, 0, L_)
    hi = jnp.clip(t - win // 2 + win, 0, L_)
    gi = jnp.arange(POOL_GROUPS)
    mean = (cs[:, hi, gi] - cs[:, lo, gi]) / (hi - lo).astype(F32)[None, :, :, None]
    p = (mean - uf).astype(u.dtype)
    y = jnp.einsum('blgc,gcd->blgd', p, w_pool).reshape(B_, L_, POOL_W)
    return y * pool_scale


def diag_scan(a_bar, bu, h0, reverse):
    edge = -1 if reverse else 0
    bu = bu.at[:, edge].add(a_bar * h0)
    a = jnp.broadcast_to(a_bar, bu.shape)

    def combine(e1, e2):
        a1, b1 = e1
        a2, b2 = e2
        return a1 * a2, a2 * b1 + b2

    _, h = lax.associative_scan(combine, (a, bu), reverse=reverse, axis=1)
    return h


def s5_mix(u_l, u_c, a_re, a_im, log_dt, b_re, b_im, c_re, c_im, d, w_glu, with_ctx_out):
    a = lax.complex(a_re.astype(F32), a_im.astype(F32))
    a_bar = jnp.exp(a * jnp.exp(log_dt.astype(F32))[..., None])
    b_bar = ((a_bar - 1.0) / a)[..., None] * lax.complex(b_re.astype(F32), b_im.astype(F32))
    cm = lax.complex(c_re.astype(F32), c_im.astype(F32))

    def drive(u):
        B_, L_ = u.shape[:2]
        ug = u.reshape(B_, L_, SSM_GROUPS, SSM_CH).astype(F32).astype(jnp.complex64)
        return jnp.einsum('blgp,sgnp->sblgn', ug, b_bar)

    bu_c = drive(u_c)
    bu_l = drive(u_l)
    zero = jnp.zeros((u_c.shape[0], SSM_GROUPS, SSM_N), jnp.complex64)
    hf_c = diag_scan(a_bar[0], bu_c[0], zero, False)
    hb_c = diag_scan(a_bar[1], bu_c[1], zero, True)
    hf_l = diag_scan(a_bar[0], bu_l[0], hf_c[:, -1], False)
    hb_l = diag_scan(a_bar[1], bu_l[1], hb_c[:, 0], True)

    def readout(hf, hb, u):
        B_, L_ = u.shape[:2]
        y = (jnp.einsum('blgn,gpn->blgp', hf, cm[0]) + jnp.einsum('blgn,gpn->blgp', hb, cm[1])).real
        y = y.reshape(B_, L_, SSM_W).astype(u.dtype) + d * u
        y = jax.nn.gelu(y)
        z = y @ w_glu
        return z[..., :SSM_W] * jax.nn.sigmoid(z[..., SSM_W:])

    y_l = readout(hf_l, hb_l, u_l)
    y_c = readout(hf_c, hb_c, u_c) if with_ctx_out else None
    return y_l, y_c


def gmlp_mix(p, ln_g, ln_b, w_s, b_s):
    B_, L_, _ = p.shape
    uv = jax.nn.gelu(p)
    u = uv[..., :GMLP_W]
    v = layernorm(uv[..., GMLP_W:], ln_g, ln_b)
    vc = v.reshape(B_, L_ // CHUNK, CHUNK, GMLP_HEADS, GMLP_W // GMLP_HEADS)
    sv = jnp.einsum('hpq,bnqhc->bnphc', w_s, vc) + b_s.T[None, None, :, :, None]
    return u * sv.reshape(B_, L_, GMLP_W)


def neighbourhood_attention(q, k, v, k_ctx, v_ctx, rpb):
    B_, L_, H, dh = q.shape
    rows = L_ // GRID_W
    wr = min(WIN_ROWS, rows)
    wc = WIN_COLS
    qg = q.reshape(B_, rows, GRID_W, H, dh)
    kg = k.reshape(B_, rows, GRID_W, H, dh)
    vg = v.reshape(B_, rows, GRID_W, H, dh)
    ci = jnp.arange(GRID_W)
    col_start = jnp.clip(ci - wc // 2, 0, GRID_W - wc)
    col_idx = col_start[:, None] + jnp.arange(wc)[None, :]
    col_off = col_idx - ci[:, None] + (WIN_COLS - 1)
    n_nb = wr * wc

    def row_block(r):
        rs = jnp.clip(r - wr // 2, 0, rows - wr)
        k_rows = lax.dynamic_slice_in_dim(kg, rs, wr, axis=1)
        v_rows = lax.dynamic_slice_in_dim(vg, rs, wr, axis=1)
        k_nb = k_rows[:, :, col_idx]
        v_nb = v_rows[:, :, col_idx]
        q_r = lax.dynamic_index_in_dim(qg, r, axis=1, keepdims=False)
        row_off = rs + jnp.arange(wr) - r + (WIN_ROWS - 1)
        bias = rpb[:, row_off[None, :, None], col_off[:, None, :]]
        s_nb = jnp.einsum('bqhd,bjqkhd->bhqjk', q_r, k_nb).astype(F32) + bias.astype(F32)[None]
        s_ctx = jnp.einsum('bqhd,bkhd->bhqk', q_r, k_ctx).astype(F32)
        s = jnp.concatenate([s_nb.reshape(B_, H, GRID_W, n_nb), s_ctx], axis=-1)
        pr = jax.nn.softmax(s, axis=-1).astype(q.dtype)
        p_nb = pr[..., :n_nb].reshape(B_, H, GRID_W, wr, wc)
        p_ctx = pr[..., n_nb:]
        return (jnp.einsum('bhqjk,bjqkhd->bqhd', p_nb, v_nb)
                + jnp.einsum('bhqk,bkhd->bqhd', p_ctx, v_ctx))

    out = lax.map(row_block, jnp.arange(rows))
    return out.transpose(1, 0, 2, 3, 4).reshape(B_, L_, H * dh)


def context_attention(q, k, v):
    B_, Lc, H, dh = q.shape
    s = jnp.einsum('bqhd,bkhd->bhqk', q, k).astype(F32)
    pr = jax.nn.softmax(s, axis=-1).astype(q.dtype)
    return jnp.einsum('bhqk,bkhd->bqhd', pr, v).reshape(B_, Lc, H * dh)


def split_in(p):
    o1 = POOL_W
    o2 = o1 + SSM_W
    o3 = o2 + 2 * GMLP_W
    B_, L_ = p.shape[:2]
    qkv = p[..., o3:].reshape(B_, L_, 3, NA_HEADS, HEAD_DIM)
    return p[..., :o1], p[..., o1:o2], p[..., o2:o3], qkv[:, :, 0], qkv[:, :, 1], qkv[:, :, 2]


def merge_branches(h, branches, w_bg, b_bg, w_br, w_out):
    br = jnp.stack(branches, axis=-2)
    proj = jnp.einsum('blic,icd->blid', br, w_br)
    gates = jax.nn.sigmoid(h @ w_bg + b_bg).reshape(proj.shape)
    return jnp.sum(gates * proj, axis=-2) @ w_out


def token_mixers(hl, hc, w_in, w_bg, b_bg, w_br, w_out, pool_w, pool_scale,
                 a_re, a_im, log_dt, b_re, b_im, c_re, c_im, d, w_glu,
                 ln_g, ln_b, w_s, b_s, rpb, with_ctx_out):
    pa_l, pb_l, pg_l, q_l, k_l, v_l = split_in(hl @ w_in)
    pa_c, pb_c, pg_c, q_c, k_c, v_c = split_in(hc @ w_in)
    qscale = HEAD_DIM ** -0.5
    q_l = axial_rope(q_l) * qscale
    k_l = axial_rope(k_l)
    ya_l = pool_mix(pa_l, pool_w, pool_scale)
    yb_l, yb_c = s5_mix(pb_l, pb_c, a_re, a_im, log_dt, b_re, b_im, c_re, c_im, d, w_glu, with_ctx_out)
    yg_l = gmlp_mix(pg_l, ln_g, ln_b, w_s, b_s)
    yd_l = neighbourhood_attention(q_l, k_l, v_l, k_c, v_c, rpb)
    out_l = merge_branches(hl, [ya_l, yb_l, yg_l, yd_l], w_bg, b_bg, w_br, w_out)
    if not with_ctx_out:
        return out_l, None
    ya_c = pool_mix(pa_c, pool_w, pool_scale)
    yg_c = gmlp_mix(pg_c, ln_g, ln_b, w_s, b_s)
    yd_c = context_attention(q_c * qscale, k_c, v_c)
    out_c = merge_branches(hc, [ya_c, yb_c, yg_c, yd_c], w_bg, b_bg, w_br, w_out)
    return out_l, out_c


def moe(x, w_router, b_router, w1, b1, w2, b2):
    T, D = x.shape
    logits = (x @ w_router).astype(F32) + b_router.astype(F32)
    top_v, top_i = lax.top_k(logits, TOP_K)
    wts = jax.nn.softmax(top_v, axis=-1)
    A = T * TOP_K
    flat_e = top_i.reshape(A)
    flat_t = jnp.arange(A, dtype=jnp.int32) // TOP_K
    flat_w = wts.reshape(A)
    order = jnp.argsort(flat_e)
    e_sorted = flat_e[order]
    counts = jnp.bincount(flat_e, length=N_EXPERTS)
    padded = (counts + MOE_BLOCK - 1) // MOE_BLOCK * MOE_BLOCK
    start = jnp.cumsum(counts) - counts
    pend = jnp.cumsum(padded)
    pstart = pend - padded
    dest = pstart[e_sorted] + (jnp.arange(A) - start[e_sorted])
    n_blocks = (A + N_EXPERTS * (MOE_BLOCK - 1) + MOE_BLOCK - 1) // MOE_BLOCK
    P = n_blocks * MOE_BLOCK
    tok_sorted = flat_t[order]
    row_tok = jnp.full((P,), T, jnp.int32).at[dest].set(tok_sorted)
    x_pad = jnp.concatenate([x, jnp.zeros((1, D), x.dtype)], axis=0)
    xb = x_pad[row_tok].reshape(n_blocks, MOE_BLOCK, D)
    block_e = jnp.minimum(jnp.searchsorted(pend, jnp.arange(n_blocks) * MOE_BLOCK, side='right'), N_EXPERTS - 1)

    def expert_block(args):
        xblk, e = args
        h = xblk @ w1[e] + b1[e]
        g = jnp.minimum(h[..., :D_EXPERT], SWIGLU_LIMIT)
        lin = jnp.clip(h[..., D_EXPERT:], -SWIGLU_LIMIT, SWIGLU_LIMIT)
        act = g * jax.nn.sigmoid(SWIGLU_ALPHA * g) * (lin + 1.0)
        return act @ w2[e] + b2[e]

    yb = lax.map(expert_block, (xb, block_e)).reshape(P, D)
    y_assign = yb[dest] * flat_w[order][:, None].astype(yb.dtype)
    return jnp.zeros((T, D), x.dtype).at[tok_sorted].add(y_assign)


def setup_inputs(seed: int = 0) -> dict:
    key = jax.random.key(seed)
    ks = iter(jax.random.split(key, 48))
    D = D_MODEL

    def nrm(shape, scale):
        return scale * jax.random.normal(next(ks), shape, F32)

    a_im_base = math.pi * jnp.arange(SSM_N, dtype=F32)
    return {
        'x': nrm((BATCH, SEQ, D), 1.0),
        'c': nrm((BATCH, D), 1.0),
        'ctx': nrm((BATCH, CTX_LEN, D), 1.0),
        'c_ctx': nrm((D,), 1.0),
        'ada_w': nrm((DEPTH, D, 6 * D), 0.5 * D ** -0.5),
        'ada_b': nrm((DEPTH, 6 * D), 0.02),
        'norm_mix_pre': 1.0 + nrm((DEPTH, D), 0.02),
        'norm_mix_post': 1.0 + nrm((DEPTH, D), 0.02),
        'norm_ffn_pre': 1.0 + nrm((DEPTH, D), 0.02),
        'norm_ffn_post': 1.0 + nrm((DEPTH, D), 0.02),
        'w_in': nrm((DEPTH, D, IN_W), D ** -0.5),
        'w_branch_gate': nrm((DEPTH, D, N_BRANCH * D), D ** -0.5),
        'b_branch_gate': nrm((DEPTH, N_BRANCH * D), 0.02),
        'w_branch': nrm((DEPTH, N_BRANCH, BRANCH_W, D), BRANCH_W ** -0.5),
        'w_out': nrm((DEPTH, D, D), D ** -0.5),
        'pool_w': nrm((DEPTH, POOL_GROUPS, POOL_W // POOL_GROUPS, POOL_W // POOL_GROUPS), (POOL_W // POOL_GROUPS) ** -0.5),
        'pool_scale': 1.0 + nrm((DEPTH, POOL_W), 0.02),
        'ssm_a_re': -0.5 + nrm((DEPTH, 2, SSM_GROUPS, SSM_N), 0.01),
        'ssm_a_im': a_im_base + nrm((DEPTH, 2, SSM_GROUPS, SSM_N), 0.01),
        'ssm_log_dt': jax.random.uniform(next(ks), (DEPTH, 2, SSM_GROUPS), F32, math.log(DT_MIN), math.log(DT_MAX)),
        'ssm_b_re': nrm((DEPTH, 2, SSM_GROUPS, SSM_N, SSM_CH), SSM_CH ** -0.5),
        'ssm_b_im': nrm((DEPTH, 2, SSM_GROUPS, SSM_N, SSM_CH), SSM_CH ** -0.5),
        'ssm_c_re': nrm((DEPTH, 2, SSM_GROUPS, SSM_CH, SSM_N), 0.5),
        'ssm_c_im': nrm((DEPTH, 2, SSM_GROUPS, SSM_CH, SSM_N), 0.5),
        'ssm_d': nrm((DEPTH, SSM_W), 1.0),
        'ssm_w_glu': nrm((DEPTH, SSM_W, 2 * SSM_W), SSM_W ** -0.5),
        'gmlp_ln_g': 1.0 + nrm((DEPTH, GMLP_W), 0.02),
        'gmlp_ln_b': nrm((DEPTH, GMLP_W), 0.02),
        'gmlp_w_s': nrm((DEPTH, GMLP_HEADS, CHUNK, CHUNK), CHUNK ** -0.5),
        'gmlp_b_s': 1.0 + nrm((DEPTH, GMLP_HEADS, CHUNK), 0.02),
        'na_rpb': nrm((DEPTH, NA_HEADS, 2 * WIN_ROWS - 1, 2 * WIN_COLS - 1), 0.1),
        'router_w': nrm((DEPTH, D, N_EXPERTS), D ** -0.5),
        'router_b': nrm((DEPTH, N_EXPERTS), 0.01),
        'expert_w1': nrm((DEPTH, N_EXPERTS, D, 2 * D_EXPERT), D ** -0.5),
        'expert_b1': nrm((DEPTH, N_EXPERTS, 2 * D_EXPERT), 0.02),
        'expert_w2': nrm((DEPTH, N_EXPERTS, D_EXPERT, D), D_EXPERT ** -0.5),
        'expert_b2': nrm((DEPTH, N_EXPERTS, D), 0.02),
    }


def reference(x, c, ctx, c_ctx, ada_w, ada_b, norm_mix_pre, norm_mix_post, norm_ffn_pre, norm_ffn_post,
              w_in, w_branch_gate, b_branch_gate, w_branch, w_out, pool_w, pool_scale,
              ssm_a_re, ssm_a_im, ssm_log_dt, ssm_b_re, ssm_b_im, ssm_c_re, ssm_c_im, ssm_d, ssm_w_glu,
              gmlp_ln_g, gmlp_ln_b, gmlp_w_s, gmlp_b_s, na_rpb,
              router_w, router_b, expert_w1, expert_b1, expert_w2, expert_b2):
    B_, L_, D = x.shape
    Lc = ctx.shape[1]
    xl, xc = x, ctx
    silu_c = jax.nn.silu(c)
    silu_cc = jax.nn.silu(c_ctx)
    for l in range(DEPTH):
        last = l == DEPTH - 1
        mod_l = (silu_c @ ada_w[l] + ada_b[l])[:, None, :]
        mod_c = (silu_cc @ ada_w[l] + ada_b[l])[None, None, :]
        sh_m_l, sc_m_l, g_m_l, sh_f_l, sc_f_l, g_f_l = jnp.split(mod_l, 6, axis=-1)
        sh_m_c, sc_m_c, g_m_c, sh_f_c, sc_f_c, g_f_c = jnp.split(mod_c, 6, axis=-1)

        hl = modulate(rmsnorm(xl, norm_mix_pre[l]), sh_m_l, sc_m_l)
        hc = modulate(rmsnorm(xc, norm_mix_pre[l]), sh_m_c, sc_m_c)
        yl, yc = token_mixers(hl, hc, w_in[l], w_branch_gate[l], b_branch_gate[l], w_branch[l], w_out[l],
                              pool_w[l], pool_scale[l],
                              ssm_a_re[l], ssm_a_im[l], ssm_log_dt[l], ssm_b_re[l], ssm_b_im[l],
                              ssm_c_re[l], ssm_c_im[l], ssm_d[l], ssm_w_glu[l],
                              gmlp_ln_g[l], gmlp_ln_b[l], gmlp_w_s[l], gmlp_b_s[l], na_rpb[l],
                              not last)
        xl = xl + g_m_l * rmsnorm(yl, norm_mix_post[l])
        if not last:
            xc = xc + g_m_c * rmsnorm(yc, norm_mix_post[l])

        hl = modulate(rmsnorm(xl, norm_ffn_pre[l]), sh_f_l, sc_f_l)
        if last:
            yl = moe(hl.reshape(B_ * L_, D), router_w[l], router_b[l], expert_w1[l], expert_b1[l],
                     expert_w2[l], expert_b2[l]).reshape(B_, L_, D)
        else:
            hc = modulate(rmsnorm(xc, norm_ffn_pre[l]), sh_f_c, sc_f_c)
            tok = jnp.concatenate([hl.reshape(B_ * L_, D), hc.reshape(B_ * Lc, D)], axis=0)
            y = moe(tok, router_w[l], router_b[l], expert_w1[l], expert_b1[l], expert_w2[l], expert_b2[l])
            yl = y[:B_ * L_].reshape(B_, L_, D)
            yc = y[B_ * L_:].reshape(B_, Lc, D)
            xc = xc + g_f_c * rmsnorm(yc, norm_ffn_post[l])
        xl = xl + g_f_l * rmsnorm(yl, norm_ffn_post[l])
    return xl
```

```python
import functools
import math

import jax
import jax.numpy as jnp
import jax.scipy.linalg
from jax import lax
from jax.experimental import pallas as pl
from jax.experimental.pallas import tpu as pltpu

F32 = jnp.float32
BF16 = jnp.bfloat16
I32 = jnp.int32

GRID_W = 64
EPS = 1e-6
POOL_GROUPS = 4
POOL_WINDOWS = (2, 4, 8, 16)
SSM_CH = 16
SSM_N = 64
GMLP_HEADS = 4
CHUNK = 128
NA_HEADS = 4
HEAD_DIM = 64
WIN_ROWS = 8
WIN_COLS = 16
ROPE_BASE = 10000.0
BRANCH_W = 256
N_BRANCH = 4
TOP_K = 4
SWIGLU_LIMIT = 7.0
SWIGLU_ALPHA = 1.702

TM = 512
TP = 256
POOL_HALO = 16
SSM_T = 32
MOE_BM = 256
LANES = 128
NEG = -1e30
VMEM_LIMIT = 56 * 1024 * 1024


def _cparams(sem):
    return pltpu.CompilerParams(dimension_semantics=sem, vmem_limit_bytes=VMEM_LIMIT)


def _dot(a, b):
    return jnp.dot(a, b, preferred_element_type=F32)


def _dot_nt(a, b):
    return lax.dot_general(a, b, (((1,), (1,)), ((), ())), preferred_element_type=F32)


def _ada_kernel(c_ref, w_ref, b_ref, o_ref):
    c = c_ref[...]
    s = c * jax.nn.sigmoid(c)
    o_ref[0] = jnp.dot(s, w_ref[0], preferred_element_type=F32,
                       precision=lax.Precision.HIGHEST) + b_ref[0]


def ada_modulation(cvec, ada_w, ada_b):
    depth, d, n = ada_w.shape
    tn = 1536
    return pl.pallas_call(
        _ada_kernel,
        grid=(depth, n // tn),
        in_specs=[
            pl.BlockSpec((8, d), lambda l, j: (0, 0)),
            pl.BlockSpec((1, d, tn), lambda l, j: (l, 0, j)),
            pl.BlockSpec((1, 1, tn), lambda l, j: (l, 0, j)),
        ],
        out_specs=pl.BlockSpec((1, 8, tn), lambda l, j: (l, 0, j)),
        out_shape=jax.ShapeDtypeStruct((depth, 8, n), F32),
        compiler_params=_cparams(("arbitrary", "arbitrary")),
        name="ada_modulation",
    )(cvec, ada_w, ada_b.reshape(depth, 1, n))


def _rms(x, g):
    return x * lax.rsqrt(jnp.mean(x * x, axis=-1, keepdims=True) + EPS) * g


def _premix_kernel(n_lat_tiles, x_ref, mod_ref, g_ref, w_ref, cos_ref, sin_ref,
                   p_ref, q_ref, k_ref, v_ref):
    j = pl.program_id(0)
    x = x_ref[...]
    h = _rms(x, g_ref[...])
    h = h * (1.0 + mod_ref[0, 1:2, :]) + mod_ref[0, 0:1, :]
    p = _dot(h.astype(BF16), w_ref[...])
    wa = p_ref.shape[1]
    p_ref[...] = p[:, :wa]
    q = p[:, wa:wa + 256]
    k = p[:, wa + 256:wa + 512]
    v = p[:, wa + 512:wa + 768]
    is_lat = j < n_lat_tiles
    cos = jnp.where(is_lat, cos_ref[...], 1.0)
    sin = jnp.where(is_lat, sin_ref[...], 0.0)
    lane = lax.broadcasted_iota(I32, q.shape, 1)
    first = (lane % 32) < 16

    def rope(t):
        partner = jnp.where(first, pltpu.roll(t, 256 - 16, axis=1), pltpu.roll(t, 16, axis=1))
        return t * cos + partner * sin

    q_ref[...] = (rope(q) * (HEAD_DIM ** -0.5)).astype(BF16)
    k_ref[...] = rope(k).astype(BF16)
    v_ref[...] = v.astype(BF16)


def premix(x, mod, gain, w_in_bf, cos_t, sin_t, n_lat, seq_len):
    nt, d = x.shape
    in_w = w_in_bf.shape[1]
    wa = in_w - 768
    n_lat_tiles = n_lat // TM
    tiles_per_seq = seq_len // TM

    def mod_row(j):
        return jnp.where(j < n_lat_tiles, j // tiles_per_seq, 4)

    def tab_blk(j):
        return jnp.where(j < n_lat_tiles, j % tiles_per_seq, 0)

    return pl.pallas_call(
        functools.partial(_premix_kernel, n_lat_tiles),
        grid=(nt // TM,),
        in_specs=[
            pl.BlockSpec((TM, d), lambda j: (j, 0)),
            pl.BlockSpec((1, 6, d), lambda j: (mod_row(j), 0, 0)),
            pl.BlockSpec((1, d), lambda j: (0, 0)),
            pl.BlockSpec((d, in_w), lambda j: (0, 0)),
            pl.BlockSpec((TM, 256), lambda j: (tab_blk(j), 0)),
            pl.BlockSpec((TM, 256), lambda j: (tab_blk(j), 0)),
        ],
        out_specs=[
            pl.BlockSpec((TM, wa), lambda j: (j, 0)),
            pl.BlockSpec((TM, 256), lambda j: (j, 0)),
            pl.BlockSpec((TM, 256), lambda j: (j, 0)),
            pl.BlockSpec((TM, 256), lambda j: (j, 0)),
        ],
        out_shape=[
            jax.ShapeDtypeStruct((nt, wa), F32),
            jax.ShapeDtypeStruct((nt, 256), BF16),
            jax.ShapeDtypeStruct((nt, 256), BF16),
            jax.ShapeDtypeStruct((nt, 256), BF16),
        ],
        compiler_params=_cparams(("arbitrary",)),
        name="premix",
    )(x, mod, gain, w_in_bf, cos_t, sin_t)


def rope_tables(seq_len):
    t = jnp.arange(seq_len)
    half = HEAD_DIM // 2
    nf = half // 2
    inv = ROPE_BASE ** (-jnp.arange(nf, dtype=F32) / nf)
    d = jnp.arange(HEAD_DIM)
    pos = jnp.where((d // half)[None, :] == 0, (t // GRID_W)[:, None], (t % GRID_W)[:, None]).astype(F32)
    ang = pos * inv[(d % half) % nf][None, :]
    sign = jnp.where((d % half) < nf, -1.0, 1.0)[None, :]
    cos = jnp.tile(jnp.cos(ang), (1, NA_HEADS))
    sin = jnp.tile(jnp.sin(ang) * sign, (1, NA_HEADS))
    return cos.astype(F32), sin.astype(F32)


def _pool_kernel(n_lat_tiles, tiles_per_seq, ctx_tiles_per_seq, cur_ref, prev_ref, next_ref,
                 w_ref, sc_ref, o_ref, ext_ref):
    j = pl.program_id(0)
    is_lat = j < n_lat_tiles
    t0 = jnp.where(is_lat, (j % tiles_per_seq) * TP, ((j - n_lat_tiles) % ctx_tiles_per_seq) * TP)
    slen = jnp.where(is_lat, tiles_per_seq * TP, ctx_tiles_per_seq * TP)
    u = cur_ref[...]
    h = POOL_HALO
    ext_ref[0:h, :] = jnp.where(t0 > 0, prev_ref[...], 0.0)
    ext_ref[h:h + TP, :] = u
    ext_ref[h + TP:h + TP + h, :] = jnp.where(t0 + TP < slen, next_ref[...], 0.0)
    t = t0 + lax.broadcasted_iota(I32, (TP, 1), 0)
    lane = lax.broadcasted_iota(I32, (TP, u.shape[1]), 1)
    grp = lane // (u.shape[1] // POOL_GROUPS)
    mean = jnp.zeros_like(u)
    for gi, w in enumerate(POOL_WINDOWS):
        acc = ext_ref[pl.ds(h - w // 2, TP), :]
        for o in range(1, w):
            acc = acc + ext_ref[pl.ds(h - w // 2 + o, TP), :]
        cnt = (jnp.minimum(t - w // 2 + w, slen) - jnp.maximum(t - w // 2, 0)).astype(F32)
        mean = jnp.where(grp == gi, acc / cnt, mean)
    y = _dot((mean - u).astype(BF16), w_ref[...]) * sc_ref[...]
    o_ref[...] = y.astype(BF16)


def pool_mix(p1, w_bd_bf, scale, n_rows, n_lat, seq_len, ctx_len):
    nt = n_rows
    hb = TP // POOL_HALO
    n_halo_blocks = p1.shape[0] // POOL_HALO
    return pl.pallas_call(
        functools.partial(_pool_kernel, n_lat // TP, seq_len // TP, ctx_len // TP),
        grid=(nt // TP,),
        in_specs=[
            pl.BlockSpec((TP, 256), lambda j: (j, 0)),
            pl.BlockSpec((POOL_HALO, 256), lambda j: (jnp.maximum(j * hb - 1, 0), 0)),
            pl.BlockSpec((POOL_HALO, 256), lambda j: (jnp.minimum((j + 1) * hb, n_halo_blocks - 1), 0)),
            pl.BlockSpec((256, 256), lambda j: (0, 0)),
            pl.BlockSpec((1, 256), lambda j: (0, 0)),
        ],
        out_specs=pl.BlockSpec((TP, 256), lambda j: (j, 0)),
        out_shape=jax.ShapeDtypeStruct((p1.shape[0], 256), BF16),
        scratch_shapes=[pltpu.VMEM((TP + 2 * POOL_HALO, 256), F32)],
        compiler_params=_cparams(("arbitrary",)),
        name="pool_mix",
    )(p1, p1, p1, w_bd_bf, scale)


def _gmlp_kernel(p_ref, g_ref, b_ref, ws_ref, bs_ref, o_ref):
    uv = jax.nn.gelu(p_ref[...])
    w = uv.shape[1] // 2
    u = uv[:, :w]
    v = uv[:, w:]
    mu = jnp.mean(v, axis=-1, keepdims=True)
    var = jnp.mean((v - mu) ** 2, axis=-1, keepdims=True)
    v = ((v - mu) * lax.rsqrt(var + EPS) * g_ref[...] + b_ref[...]).astype(BF16)
    lane = lax.broadcasted_iota(I32, (CHUNK, w), 1)
    head = lane // (w // GMLP_HEADS)
    for c in range(TP // CHUNK):
        vc = v[c * CHUNK:(c + 1) * CHUNK]
        sv = bs_ref[...]
        for hh in range(GMLP_HEADS):
            sv = sv + jnp.where(head == hh, _dot(ws_ref[hh], vc), 0.0)
        o_ref[c * CHUNK:(c + 1) * CHUNK, :] = (u[c * CHUNK:(c + 1) * CHUNK] * sv).astype(BF16)


def gmlp_mix(p1, ln_g, ln_b, ws_bf, bs_full, n_rows):
    return pl.pallas_call(
        _gmlp_kernel,
        grid=(n_rows // TP,),
        in_specs=[
            pl.BlockSpec((TP, 512), lambda j: (j, 1)),
            pl.BlockSpec((1, 256), lambda j: (0, 0)),
            pl.BlockSpec((1, 256), lambda j: (0, 0)),
            pl.BlockSpec((GMLP_HEADS, CHUNK, CHUNK), lambda j: (0, 0, 0)),
            pl.BlockSpec((CHUNK, 256), lambda j: (0, 0)),
        ],
        out_specs=pl.BlockSpec((TP, 256), lambda j: (j, 0)),
        out_shape=jax.ShapeDtypeStruct((p1.shape[0], 256), BF16),
        compiler_params=_cparams(("arbitrary",)),
        name="gmlp_mix",
    )(p1, ln_g, ln_b, ws_bf, bs_full)


def s5_tables(a_re, a_im, log_dt, b_re, b_im, c_re, c_im):
    hp = lax.Precision.HIGHEST
    T = SSM_T
    a = lax.complex(a_re.astype(F32), a_im.astype(F32))
    adt = a * jnp.exp(log_dt.astype(F32))[..., None]
    a_bar = jnp.exp(adt)
    b_bar = ((a_bar - 1.0) / a)[..., None] * lax.complex(b_re.astype(F32), b_im.astype(F32))
    cm = lax.complex(c_re.astype(F32), c_im.astype(F32))
    kk = jnp.arange(T + 1, dtype=F32)
    pw = jnp.exp(adt[None] * kk[:, None, None, None].astype(jnp.complex64))
    cb_re = jnp.einsum('dgpn,kdgn,dgnq->dkgpq', cm, pw[:T], b_bar, precision=hp).real
    s_idx = jnp.arange(T)
    lag = s_idx[None, :] - s_idx[:, None]
    kf = cb_re[0][jnp.clip(lag, 0, T - 1)]
    kb = cb_re[1][jnp.clip(-lag, 0, T - 1)]
    toe = jnp.where((lag >= 0)[:, :, None, None, None], kf, 0.0) + \
        jnp.where((lag <= 0)[:, :, None, None, None], kb, 0.0)
    G = toe.shape[2]
    P = toe.shape[3]
    toe = toe.transpose(2, 4, 0, 3, 1).reshape(G, P * T, P * T)
    sf = pw[T - 1 - s_idx, 0][:, :, :, None] * b_bar[0][None]
    sb = pw[s_idx, 1][:, :, :, None] * b_bar[1][None]

    def pack_s(z):
        z = z.transpose(1, 3, 0, 2).reshape(G, P * T, -1)
        return z.real, z.imag

    sfr, sfi = pack_s(sf)
    sbr, sbi = pack_s(sb)
    sall = jnp.concatenate([sfr, sfi, sfi, sfr, sbr, sbi, sbi, sbr], axis=-1)
    wf = cm[0][:, :, :, None] * pw[1 + s_idx, 0].transpose(1, 2, 0)[:, None, :, :]
    wb = cm[1][:, :, :, None] * pw[T - s_idx, 1].transpose(1, 2, 0)[:, None, :, :]

    def pack_w(z):
        z = z.transpose(0, 2, 1, 3).reshape(G, -1, P * T)
        return jnp.concatenate([z.real, -z.imag], axis=1)

    wst = jnp.concatenate([pack_w(wf), pack_w(wb)], axis=1)
    at = pw[T]
    rows = []
    for dd in range(2):
        ar, ai = at[dd].real, at[dd].imag
        rows += [jnp.concatenate([ar, ar], -1), jnp.concatenate([-ai, ai], -1), jnp.concatenate([ai, -ai], -1)]
    rows += [jnp.zeros_like(rows[0])] * 2
    dec = jnp.stack(rows, axis=1)
    return toe.astype(BF16), sall.astype(BF16), wst.astype(BF16), dec.astype(F32)


def _s5_kernel(n_ctx_chunks, n_lat_chunks, nb, a_ref, toe_ref, sall_ref, wst_ref, dec_ref,
               o_ref, x_ref, h_ref):
    a = a_ref[0].astype(BF16)
    y = _dot(a, toe_ref[0])
    x_ref[...] = _dot(a, sall_ref[0])
    dec = dec_ref[0]
    n_chunks = n_ctx_chunks + n_lat_chunks
    fwd_order = list(range(n_lat_chunks, n_chunks)) + list(range(n_lat_chunks))
    bwd_order = list(range(n_chunks - 1, n_lat_chunks - 1, -1)) + list(range(n_lat_chunks - 1, -1, -1))
    for d, order in enumerate((fwd_order, bwd_order)):
        a1 = dec[3 * d:3 * d + 1]
        a2 = dec[3 * d + 1:3 * d + 2]
        a3 = dec[3 * d + 2:3 * d + 3]
        s1 = jnp.zeros((nb, LANES), F32)
        s2 = jnp.zeros((nb, LANES), F32)
        for c in order:
            r = c * nb
            h_ref[r:r + nb, d * LANES:(d + 1) * LANES] = s1
            x1 = x_ref[r:r + nb, (2 * d) * LANES:(2 * d + 1) * LANES]
            x2 = x_ref[r:r + nb, (2 * d + 1) * LANES:(2 * d + 2) * LANES]
            s1, s2 = a1 * s1 + a2 * s2 + x1, a1 * s2 + a3 * s1 + x2
    y = y + _dot(h_ref[...].astype(BF16), wst_ref[0])
    o_ref[0] = y


def s5_core(a_g, toe, sall, wst, dec, n_ctx_chunks, n_lat_chunks, nb):
    G, rows, w = a_g.shape
    return pl.pallas_call(
        functools.partial(_s5_kernel, n_ctx_chunks, n_lat_chunks, nb),
        grid=(G,),
        in_specs=[
            pl.BlockSpec((1, rows, w), lambda g: (g, 0, 0)),
            pl.BlockSpec((1, w, w), lambda g: (g, 0, 0)),
            pl.BlockSpec((1, w, 512), lambda g: (g, 0, 0)),
            pl.BlockSpec((1, 256, w), lambda g: (g, 0, 0)),
            pl.BlockSpec((1, 8, LANES), lambda g: (g, 0, 0)),
        ],
        out_specs=pl.BlockSpec((1, rows, w), lambda g: (g, 0, 0)),
        out_shape=jax.ShapeDtypeStruct((G, rows, w), F32),
        scratch_shapes=[pltpu.VMEM((rows, 512), F32), pltpu.VMEM((rows, 256), F32)],
        compiler_params=_cparams(("arbitrary",)),
        name="s5_core",
    )(a_g, toe, sall, wst, dec)


def s5_to_groups(u, nb, seq_len, ctx_len):
    T = SSM_T
    n_lat = nb * seq_len
    G = u.shape[1] // SSM_CH
    ul = u[:n_lat].reshape(nb, seq_len // T, T, G, SSM_CH).transpose(3, 1, 0, 4, 2)
    uc = u[n_lat:].reshape(nb, ctx_len // T, T, G, SSM_CH).transpose(3, 1, 0, 4, 2)
    return jnp.concatenate([ul, uc], axis=1).reshape(G, -1, SSM_CH * T)


def s5_from_groups(y, nb, seq_len, ctx_len):
    T = SSM_T
    G = y.shape[0]
    y = y.reshape(G, -1, nb, SSM_CH, T)
    nl = seq_len // T
    yl = y[:, :nl].transpose(2, 1, 4, 0, 3).reshape(nb * seq_len, G * SSM_CH)
    yc = y[:, nl:].transpose(2, 1, 4, 0, 3).reshape(nb * ctx_len, G * SSM_CH)
    return jnp.concatenate([yl, yc], axis=0)


def _s5_post_kernel(y_ref, u_ref, d_ref, w_ref, o_ref):
    y = jax.nn.gelu(y_ref[...] + d_ref[...] * u_ref[...])
    z = _dot(y.astype(BF16), w_ref[...])
    w = z.shape[1] // 2
    o_ref[...] = (z[:, :w] * jax.nn.sigmoid(z[:, w:])).astype(BF16)


def s5_post(y_nat, p1, d, w_glu_bf):
    nt = y_nat.shape[0]
    return pl.pallas_call(
        _s5_post_kernel,
        grid=(nt // TM,),
        in_specs=[
            pl.BlockSpec((TM, 256), lambda j: (j, 0)),
            pl.BlockSpec((TM, 256), lambda j: (j, 1)),
            pl.BlockSpec((1, 256), lambda j: (0, 0)),
            pl.BlockSpec((256, 512), lambda j: (0, 0)),
        ],
        out_specs=pl.BlockSpec((TM, 256), lambda j: (j, 0)),
        out_shape=jax.ShapeDtypeStruct((nt, 256), BF16),
        compiler_params=_cparams(("arbitrary",)),
        name="s5_post",
    )(y_nat, p1, d, w_glu_bf)


def na_bias_table(rpb):
    ci = jnp.arange(GRID_W)
    cs = jnp.clip(ci - WIN_COLS // 2, 0, GRID_W - WIN_COLS)
    kc = jnp.arange(GRID_W)
    inside = (kc[None, :] >= cs[:, None]) & (kc[None, :] < cs[:, None] + WIN_COLS)
    col_off = jnp.clip(kc[None, :] - ci[:, None] + (WIN_COLS - 1), 0, 2 * WIN_COLS - 2)
    ty = jnp.arange(WIN_ROWS)
    jj = jnp.arange(WIN_ROWS)
    row_off = jj[None, :] - ty[:, None] + (WIN_ROWS - 1)
    tab = rpb.astype(F32)[:, row_off[:, :, None, None], col_off[None, None, :, :]]
    tab = jnp.where(inside[None, None, None], tab, NEG)
    tab = tab.transpose(1, 0, 3, 2, 4)
    return tab.reshape(WIN_ROWS, NA_HEADS * GRID_W, WIN_ROWS * GRID_W)


def _heads_stack(q):
    lane = lax.broadcasted_iota(I32, q.shape, 1)
    zero = jnp.zeros_like(q)
    return jnp.concatenate([jnp.where(lane // HEAD_DIM == hh, q, zero) for hh in range(NA_HEADS)], axis=0)


def _heads_merge(o4, m):
    lane = lax.broadcasted_iota(I32, (m, o4.shape[1]), 1)
    out = jnp.zeros((m, o4.shape[1]), F32)
    for hh in range(NA_HEADS):
        out = jnp.where(lane // HEAD_DIM == hh, o4[hh * m:(hh + 1) * m], out)
    return out


def _na_kernel(rows_per_step, n_rows, q_ref, k_ref, v_ref, kc_ref, vc_ref, tab_ref, o_ref):
    i = pl.program_id(1)
    kc = kc_ref[...]
    vc = vc_ref[...]
    nkeys = WIN_ROWS * GRID_W

    def body(a, carry):
        r = i * rows_per_step + a
        rs = jnp.clip(r - WIN_ROWS // 2, 0, n_rows - WIN_ROWS)
        ty = r - rs
        q0 = pl.multiple_of(a * GRID_W, GRID_W)
        k0 = pl.multiple_of(rs * GRID_W, GRID_W)
        q4 = _heads_stack(q_ref[pl.ds(q0, GRID_W), :])
        kw = k_ref[pl.ds(k0, nkeys), :]
        vw = v_ref[pl.ds(k0, nkeys), :]
        s = _dot_nt(q4, kw) + tab_ref[ty]
        sc = _dot_nt(q4, kc)
        m = jnp.maximum(jnp.max(s, axis=-1, keepdims=True), jnp.max(sc, axis=-1, keepdims=True))
        e = jnp.exp(s - m)
        ec = jnp.exp(sc - m)
        den = jnp.sum(e, axis=-1, keepdims=True) + jnp.sum(ec, axis=-1, keepdims=True)
        o4 = (_dot(e.astype(BF16), vw) + _dot(ec.astype(BF16), vc)) / den
        o_ref[pl.ds(q0, GRID_W), :] = _heads_merge(o4, GRID_W).astype(BF16)
        return carry

    lax.fori_loop(0, rows_per_step, body, 0)


def na_latent(q, k, v, tab, nb, seq_len, ctx_len, n_rows_total):
    rows_per_step = TM // GRID_W
    n_rows = seq_len // GRID_W
    steps = n_rows // rows_per_step
    ctx_blk0 = nb * seq_len // ctx_len
    return pl.pallas_call(
        functools.partial(_na_kernel, rows_per_step, n_rows),
        grid=(nb, steps),
        in_specs=[
            pl.BlockSpec((TM, 256), lambda b, i: (b * steps + i, 0)),
            pl.BlockSpec((seq_len, 256), lambda b, i: (b, 0)),
            pl.BlockSpec((seq_len, 256), lambda b, i: (b, 0)),
            pl.BlockSpec((ctx_len, 256), lambda b, i: (ctx_blk0 + b, 0)),
            pl.BlockSpec((ctx_len, 256), lambda b, i: (ctx_blk0 + b, 0)),
            pl.BlockSpec(tab.shape, lambda b, i: (0, 0, 0)),
        ],
        out_specs=pl.BlockSpec((TM, 256), lambda b, i: (b * steps + i, 0)),
        out_shape=jax.ShapeDtypeStruct((n_rows_total, 256), BF16),
        compiler_params=_cparams(("arbitrary", "arbitrary")),
        name="na_latent",
    )(q, k, v, k, v, tab)


def _ctx_attn_kernel(q_ref, k_ref, v_ref, prev_ref, o_ref):
    del prev_ref
    m_q = q_ref.shape[0]
    q4 = _heads_stack(q_ref[...])
    s = _dot_nt(q4, k_ref[...])
    m = jnp.max(s, axis=-1, keepdims=True)
    e = jnp.exp(s - m)
    den = jnp.sum(e, axis=-1, keepdims=True)
    o4 = _dot(e.astype(BF16), v_ref[...]) / den
    o_ref[...] = _heads_merge(o4, m_q).astype(BF16)


def ctx_attention(q, k, v, yd, nb, seq_len, ctx_len):
    blk0 = nb * seq_len // ctx_len
    spec = pl.BlockSpec((ctx_len, 256), lambda b: (blk0 + b, 0))
    return pl.pallas_call(
        _ctx_attn_kernel,
        grid=(nb,),
        in_specs=[spec, spec, spec, pl.BlockSpec(memory_space=pl.ANY)],
        out_specs=spec,
        out_shape=jax.ShapeDtypeStruct(yd.shape, yd.dtype),
        input_output_aliases={3: 0},
        compiler_params=_cparams(("arbitrary",)),
        name="ctx_attention",
    )(q, k, v, yd)


def _merge_kernel(tiles_per_seq, n_lat_tiles, x_ref, mod_ref, gpre_ref, gpost_ref, gffn_ref,
                  ya_ref, yb_ref, yg_ref, yd_ref, wbg_ref, bbg_ref, wbr_ref, wout_ref,
                  wrh_ref, wrl_ref, br_ref, xn_ref, hf_ref, ti_ref, tw_ref):
    x = x_ref[...]
    d = x.shape[1]
    mod = mod_ref[0]
    h = _rms(x, gpre_ref[...]) * (1.0 + mod[1:2]) + mod[0:1]
    hb = h.astype(BF16)
    acc = jnp.zeros_like(x)
    for bi, br_ref_i in enumerate((ya_ref, yb_ref, yg_ref, yd_ref)):
        gate = jax.nn.sigmoid(_dot(hb, wbg_ref[:, bi * d:(bi + 1) * d]) + bbg_ref[:, bi * d:(bi + 1) * d])
        acc = acc + gate * _dot(br_ref_i[...], wbr_ref[bi])
    y = _dot(acc.astype(BF16), wout_ref[...])
    xn = x + mod[2:3] * _rms(y, gpost_ref[...])
    xn_ref[...] = xn
    hf = _rms(xn, gffn_ref[...]) * (1.0 + mod[4:5]) + mod[3:4]
    hf_hi = hf.astype(BF16)
    hf_ref[...] = hf_hi
    hf_lo = (hf - hf_hi.astype(F32)).astype(BF16)
    logits = _dot(hf_hi, wrh_ref[...]) + (_dot(hf_hi, wrl_ref[...]) + _dot(hf_lo, wrh_ref[...])) + br_ref[...]
    lane = lax.broadcasted_iota(I32, logits.shape, 1)
    ti = jnp.zeros(logits.shape, I32)
    tv = jnp.zeros(logits.shape, F32)
    work = logits
    v0 = None
    den = None
    for kk in range(TOP_K):
        m = jnp.max(work, axis=-1, keepdims=True)
        idx = jnp.min(jnp.where(work == m, lane, LANES), axis=-1, keepdims=True)
        if kk == 0:
            v0 = m
            e = jnp.ones_like(m)
            den = e
        else:
            e = jnp.exp(m - v0)
            den = den + e
        ti = jnp.where(lane == kk, idx, ti)
        tv = jnp.where(lane == kk, e, tv)
        work = jnp.where(lane == idx, -jnp.inf, work)
    ti_ref[...] = ti
    tw_ref[...] = tv / den


def merge_and_route(x, mod, gpre, gpost, gffn, ya, yb, yg, yd, wbg, bbg, wbr, wout, wrh, wrl, br,
                    n_tokens, n_lat, seq_len):
    d = x.shape[1]
    n_lat_tiles = n_lat // TM
    tiles_per_seq = seq_len // TM

    def mod_row(j):
        return jnp.where(j < n_lat_tiles, j // tiles_per_seq, 4)

    tok = lambda w: pl.BlockSpec((TM, w), lambda j: (j, 0))
    full = lambda shape: pl.BlockSpec(shape, lambda j: tuple(0 for _ in shape))
    return pl.pallas_call(
        functools.partial(_merge_kernel, tiles_per_seq, n_lat_tiles),
        grid=(n_tokens // TM,),
        in_specs=[
            tok(d),
            pl.BlockSpec((1, 6, d), lambda j: (mod_row(j), 0, 0)),
            full((1, d)), full((1, d)), full((1, d)),
            tok(256), tok(256), tok(256), tok(256),
            full(wbg.shape), full(bbg.shape), full(wbr.shape), full(wout.shape),
            full(wrh.shape), full(wrl.shape), full(br.shape),
        ],
        out_specs=[tok(d), tok(d), tok(LANES), tok(LANES)],
        out_shape=[
            jax.ShapeDtypeStruct((n_tokens, d), F32),
            jax.ShapeDtypeStruct((n_tokens, d), BF16),
            jax.ShapeDtypeStruct((n_tokens, LANES), I32),
            jax.ShapeDtypeStruct((n_tokens, LANES), F32),
        ],
        compiler_params=_cparams(("arbitrary",)),
        name="merge_and_route",
    )(x, mod, gpre, gpost, gffn, ya, yb, yg, yd, wbg, bbg, wbr, wout, wrh, wrl, br)


def _expert_kernel(be_ref, nv_ref, x_ref, w1_ref, b1_ref, w2_ref, b2_ref, o_ref, w1b_ref, w2b_ref):
    i = pl.program_id(0)
    prev = be_ref[jnp.maximum(i - 1, 0)]
    changed = jnp.logical_or(i == 0, be_ref[i] != prev)

    @pl.when(jnp.logical_and(i < nv_ref[0], changed))
    def _():
        w1b_ref[...] = w1_ref[0].astype(BF16)
        w2b_ref[...] = w2_ref[0].astype(BF16)

    @pl.when(i < nv_ref[0])
    def _():
        h = _dot(x_ref[...], w1b_ref[...]) + b1_ref[0]
        de = h.shape[1] // 2
        g = jnp.minimum(h[:, :de], SWIGLU_LIMIT)
        lin = jnp.clip(h[:, de:], -SWIGLU_LIMIT, SWIGLU_LIMIT)
        act = g * jax.nn.sigmoid(SWIGLU_ALPHA * g) * (lin + 1.0)
        o_ref[...] = _dot(act.astype(BF16), w2b_ref[...]) + b2_ref[0]


def moe_experts(xb, block_e, n_valid, w1, b1, w2, b2):
    p, d = xb.shape
    ne, _, de2 = w1.shape
    n_blocks = p // MOE_BM
    grid_spec = pltpu.PrefetchScalarGridSpec(
        num_scalar_prefetch=2,
        grid=(n_blocks,),
        in_specs=[
            pl.BlockSpec((MOE_BM, d), lambda i, be, nv: (i, 0)),
            pl.BlockSpec((1, d, de2), lambda i, be, nv: (be[i], 0, 0)),
            pl.BlockSpec((1, 1, de2), lambda i, be, nv: (be[i], 0, 0)),
            pl.BlockSpec((1, de2 // 2, d), lambda i, be, nv: (be[i], 0, 0)),
            pl.BlockSpec((1, 1, d), lambda i, be, nv: (be[i], 0, 0)),
        ],
        out_specs=pl.BlockSpec((MOE_BM, d), lambda i, be, nv: (i, 0)),
        scratch_shapes=[pltpu.VMEM((d, de2), BF16), pltpu.VMEM((de2 // 2, d), BF16)],
    )
    return pl.pallas_call(
        _expert_kernel,
        grid_spec=grid_spec,
        out_shape=jax.ShapeDtypeStruct((p, d), F32),
        compiler_params=_cparams(("arbitrary",)),
        name="moe_experts",
    )(block_e, n_valid, xb, w1, b1.reshape(ne, 1, de2), w2, b2.reshape(ne, 1, d))


def _combine_kernel(x_ref, mod_ref, g_ref, yg_ref, tw_ref, o_ref):
    tw = tw_ref[...]
    y = tw[:, 0:1] * yg_ref[0]
    for kk in range(1, TOP_K):
        y = y + tw[:, kk:kk + 1] * yg_ref[kk]
    o_ref[...] = x_ref[...] + mod_ref[0, 5:6, :] * _rms(y, g_ref[...])


def moe_combine(xn, mod, gain, yg, tw, n_tokens, n_lat, seq_len):
    d = xn.shape[1]
    n_lat_tiles = n_lat // TM
    tiles_per_seq = seq_len // TM

    def mod_row(j):
        return jnp.where(j < n_lat_tiles, j // tiles_per_seq, 4)

    return pl.pallas_call(
        _combine_kernel,
        grid=(n_tokens // TM,),
        in_specs=[
            pl.BlockSpec((TM, d), lambda j: (j, 0)),
            pl.BlockSpec((1, 6, d), lambda j: (mod_row(j), 0, 0)),
            pl.BlockSpec((1, d), lambda j: (0, 0)),
            pl.BlockSpec((TOP_K, TM, d), lambda j: (0, j, 0)),
            pl.BlockSpec((TM, LANES), lambda j: (j, 0)),
        ],
        out_specs=pl.BlockSpec((TM, d), lambda j: (j, 0)),
        out_shape=jax.ShapeDtypeStruct((n_tokens, d), F32),
        compiler_params=_cparams(("arbitrary",)),
        name="moe_combine",
    )(xn, mod, gain, yg, tw)


def moe_plan(top_i, n_experts):
    t = top_i.shape[0]
    a = t * TOP_K
    flat_e = top_i.reshape(a)
    onehot = (flat_e[:, None] == jnp.arange(n_experts, dtype=I32)[None, :]).astype(I32)
    csum = jnp.cumsum(onehot, axis=0)
    counts = csum[-1]
    rank = jnp.take_along_axis(csum, flat_e[:, None], axis=1)[:, 0] - 1
    padded = (counts + MOE_BM - 1) // MOE_BM * MOE_BM
    pend = jnp.cumsum(padded)
    pstart = pend - padded
    dest = pstart[flat_e] + rank
    n_blocks = a // MOE_BM + n_experts
    row_tok = jnp.zeros((n_blocks * MOE_BM,), I32).at[dest].set(jnp.arange(a, dtype=I32) // TOP_K)
    blk_start = jnp.arange(n_blocks, dtype=I32) * MOE_BM
    block_e = jnp.minimum(jnp.searchsorted(pend, blk_start, side='right'), n_experts - 1).astype(I32)
    n_valid = (pend[-1] // MOE_BM).astype(I32).reshape(1)
    return dest.reshape(t, TOP_K), row_tok, block_e, n_valid


def kernel(x, c, ctx, c_ctx, ada_w, ada_b, norm_mix_pre, norm_mix_post, norm_ffn_pre, norm_ffn_post,
           w_in, w_branch_gate, b_branch_gate, w_branch, w_out, pool_w, pool_scale,
           ssm_a_re, ssm_a_im, ssm_log_dt, ssm_b_re, ssm_b_im, ssm_c_re, ssm_c_im, ssm_d, ssm_w_glu,
           gmlp_ln_g, gmlp_ln_b, gmlp_w_s, gmlp_b_s, na_rpb,
           router_w, router_b, expert_w1, expert_b1, expert_w2, expert_b2):
    nb, seq_len, d = x.shape
    ctx_len = ctx.shape[1]
    depth = ada_w.shape[0]
    n_lat = nb * seq_len
    n_all = n_lat + nb * ctx_len
    n_experts = router_w.shape[-1]
    assert nb == 4 and seq_len % TM == 0 and (nb * ctx_len) % TM == 0 and ctx_len % TP == 0

    xs = jnp.concatenate([x.reshape(n_lat, d), ctx.reshape(nb * ctx_len, d)], axis=0)
    cvec = jnp.concatenate([c, c_ctx[None, :], jnp.zeros((3, d), F32)], axis=0)
    mods = ada_modulation(cvec, ada_w, ada_b).reshape(depth, 8, 6, d)
    cos_t, sin_t = rope_tables(seq_len)

    out = None
    for l in range(depth):
        last = l == depth - 1
        mod = mods[l]
        row = lambda v: v.reshape(1, -1)
        p1, q, k, v = premix(xs, mod, row(norm_mix_pre[l]), w_in[l].astype(BF16), cos_t, sin_t, n_lat, seq_len)

        n_mix = n_lat if last else n_all
        w_bd = jax.scipy.linalg.block_diag(*[pool_w[l, g] for g in range(POOL_GROUPS)]).astype(BF16)
        ya = pool_mix(p1, w_bd, row(pool_scale[l]), n_mix, n_lat, seq_len, ctx_len)

        bs_full = jnp.repeat(gmlp_b_s[l].T, 256 // GMLP_HEADS, axis=1)
        yg = gmlp_mix(p1, row(gmlp_ln_g[l]), row(gmlp_ln_b[l]), gmlp_w_s[l].astype(BF16), bs_full, n_mix)

        toe, sall, wst, dec = s5_tables(ssm_a_re[l], ssm_a_im[l], ssm_log_dt[l], ssm_b_re[l], ssm_b_im[l],
                                        ssm_c_re[l], ssm_c_im[l])
        a_g = s5_to_groups(p1[:, 256:512], nb, seq_len, ctx_len)
        y_g = s5_core(a_g, toe, sall, wst, dec, ctx_len // SSM_T, seq_len // SSM_T, nb)
        y_nat = s5_from_groups(y_g, nb, seq_len, ctx_len)
        yb = s5_post(y_nat, p1, row(ssm_d[l]), ssm_w_glu[l].astype(BF16))

        tab = na_bias_table(na_rpb[l])
        yd = na_latent(q, k, v, tab, nb, seq_len, ctx_len, n_all)
        if not last:
            yd = ctx_attention(q, k, v, yd, nb, seq_len, ctx_len)

        rw = jnp.pad(router_w[l], ((0, 0), (0, LANES - n_experts)))
        rw_hi = rw.astype(BF16)
        rw_lo = (rw - rw_hi.astype(F32)).astype(BF16)
        rb = jnp.pad(router_b[l], (0, LANES - n_experts), constant_values=NEG).reshape(1, LANES)
        n_tok = n_lat if last else n_all
        xn, hf, ti, tw = merge_and_route(
            xs, mod, row(norm_mix_pre[l]), row(norm_mix_post[l]), row(norm_ffn_pre[l]),
            ya, yb, yg, yd, w_branch_gate[l].astype(BF16), row(b_branch_gate[l]),
            w_branch[l].astype(BF16), w_out[l].astype(BF16), rw_hi, rw_lo, rb, n_tok, n_lat, seq_len)

        dest, row_tok, block_e, n_valid = moe_plan(ti[:, :TOP_K], n_experts)
        xb = jnp.take(hf, row_tok, axis=0)
        yblk = moe_experts(xb, block_e, n_valid, expert_w1[l], expert_b1[l], expert_w2[l], expert_b2[l])
        ygath = jnp.take(yblk, dest.T.reshape(-1), axis=0).reshape(TOP_K, n_tok, d)
        xs = moe_combine(xn, mod, row(norm_ffn_post[l]), ygath, tw, n_tok, n_lat, seq_len)
        out = xs
    return out.reshape(nb, seq_len, d)
```

```python
import functools
import math

import jax
import jax.numpy as jnp
import jax.scipy.linalg
from jax import lax
from jax.experimental import pallas as pl
from jax.experimental.pallas import tpu as pltpu

F32 = jnp.float32
BF16 = jnp.bfloat16
I32 = jnp.int32

GRID_W = 64
EPS = 1e-6
POOL_GROUPS = 4
POOL_WINDOWS = (2, 4, 8, 16)
SSM_CH = 16
SSM_N = 64
GMLP_HEADS = 4
CHUNK = 128
NA_HEADS = 4
HEAD_DIM = 64
WIN_ROWS = 8
WIN_COLS = 16
ROPE_BASE = 10000.0
BRANCH_W = 256
N_BRANCH = 4
TOP_K = 4
SWIGLU_LIMIT = 7.0
SWIGLU_ALPHA = 1.702

TM = 512
TP = 256
POOL_HALO = 16
SSM_T = 32
MOE_BM = 256
LANES = 128
NEG = -1e30
VMEM_LIMIT = 56 * 1024 * 1024


def _cparams(sem):
    return pltpu.CompilerParams(dimension_semantics=sem, vmem_limit_bytes=VMEM_LIMIT)


def _dot(a, b):
    return jnp.dot(a, b, preferred_element_type=F32)


def _dot_nt(a, b):
    return lax.dot_general(a, b, (((1,), (1,)), ((), ())), preferred_element_type=F32)


def _ada_kernel(c_ref, w_ref, b_ref, o_ref):
    c = c_ref[...]
    s = c * jax.nn.sigmoid(c)
    o_ref[0] = jnp.dot(s, w_ref[0], preferred_element_type=F32,
                       precision=lax.Precision.HIGHEST) + b_ref[0]


def ada_modulation(cvec, ada_w, ada_b):
    depth, d, n = ada_w.shape
    tn = 1536
    return pl.pallas_call(
        _ada_kernel,
        grid=(depth, n // tn),
        in_specs=[
            pl.BlockSpec((8, d), lambda l, j: (0, 0)),
            pl.BlockSpec((1, d, tn), lambda l, j: (l, 0, j)),
            pl.BlockSpec((1, 1, tn), lambda l, j: (l, 0, j)),
        ],
        out_specs=pl.BlockSpec((1, 8, tn), lambda l, j: (l, 0, j)),
        out_shape=jax.ShapeDtypeStruct((depth, 8, n), F32),
        compiler_params=_cparams(("arbitrary", "arbitrary")),
        name="ada_modulation",
    )(cvec, ada_w, ada_b.reshape(depth, 1, n))


def _rms(x, g):
    return x * lax.rsqrt(jnp.mean(x * x, axis=-1, keepdims=True) + EPS) * g


def _premix_kernel(n_lat_tiles, x_ref, mod_ref, g_ref, w_ref, cos_ref, sin_ref,
                   pa_ref, pu_ref, pg_ref, q_ref, k_ref, v_ref):
    j = pl.program_id(0)
    x = x_ref[...]
    h = _rms(x, g_ref[...])
    h = h * (1.0 + mod_ref[0, 1:2, :]) + mod_ref[0, 0:1, :]
    p = _dot(h.astype(BF16), w_ref[...])
    wa = p.shape[1] - 768
    pa_ref[...] = p[:, 0:256]
    pu_ref[...] = p[:, 256:512]
    pg_ref[...] = p[:, 512:wa]
    q = p[:, wa:wa + 256]
    k = p[:, wa + 256:wa + 512]
    v = p[:, wa + 512:wa + 768]
    is_lat = j < n_lat_tiles
    cos = jnp.where(is_lat, cos_ref[...], 1.0)
    sin = jnp.where(is_lat, sin_ref[...], 0.0)
    lane = lax.broadcasted_iota(I32, q.shape, 1)
    first = (lane % 32) < 16

    def rope(t):
        partner = jnp.where(first, pltpu.roll(t, 256 - 16, axis=1), pltpu.roll(t, 16, axis=1))
        return t * cos + partner * sin

    q_ref[...] = (rope(q) * (HEAD_DIM ** -0.5)).astype(BF16)
    k_ref[...] = rope(k).astype(BF16)
    v_ref[...] = v.astype(BF16)


def premix(x, mod, gain, w_in_bf, cos_t, sin_t, n_lat, seq_len):
    nt, d = x.shape
    in_w = w_in_bf.shape[1]
    wa = in_w - 768
    n_lat_tiles = n_lat // TM
    tiles_per_seq = seq_len // TM

    def mod_row(j):
        return jnp.where(j < n_lat_tiles, j // tiles_per_seq, 4)

    def tab_blk(j):
        return jnp.where(j < n_lat_tiles, j % tiles_per_seq, 0)

    return pl.pallas_call(
        functools.partial(_premix_kernel, n_lat_tiles),
        grid=(nt // TM,),
        in_specs=[
            pl.BlockSpec((TM, d), lambda j: (j, 0)),
            pl.BlockSpec((1, 6, d), lambda j: (mod_row(j), 0, 0)),
            pl.BlockSpec((1, d), lambda j: (0, 0)),
            pl.BlockSpec((d, in_w), lambda j: (0, 0)),
            pl.BlockSpec((TM, 256), lambda j: (tab_blk(j), 0)),
            pl.BlockSpec((TM, 256), lambda j: (tab_blk(j), 0)),
        ],
        out_specs=[
            pl.BlockSpec((TM, 256), lambda j: (j, 0)),
            pl.BlockSpec((TM, 256), lambda j: (j, 0)),
            pl.BlockSpec((TM, wa - 512), lambda j: (j, 0)),
            pl.BlockSpec((TM, 256), lambda j: (j, 0)),
            pl.BlockSpec((TM, 256), lambda j: (j, 0)),
            pl.BlockSpec((TM, 256), lambda j: (j, 0)),
        ],
        out_shape=[
            jax.ShapeDtypeStruct((nt, 256), F32),
            jax.ShapeDtypeStruct((nt, 256), F32),
            jax.ShapeDtypeStruct((nt, wa - 512), F32),
            jax.ShapeDtypeStruct((nt, 256), BF16),
            jax.ShapeDtypeStruct((nt, 256), BF16),
            jax.ShapeDtypeStruct((nt, 256), BF16),
        ],
        compiler_params=_cparams(("arbitrary",)),
        name="premix",
    )(x, mod, gain, w_in_bf, cos_t, sin_t)


def rope_tables(seq_len):
    t = jnp.arange(seq_len)
    half = HEAD_DIM // 2
    nf = half // 2
    inv = ROPE_BASE ** (-jnp.arange(nf, dtype=F32) / nf)
    d = jnp.arange(HEAD_DIM)
    pos = jnp.where((d // half)[None, :] == 0, (t // GRID_W)[:, None], (t % GRID_W)[:, None]).astype(F32)
    ang = pos * inv[(d % half) % nf][None, :]
    sign = jnp.where((d % half) < nf, -1.0, 1.0)[None, :]
    cos = jnp.tile(jnp.cos(ang), (1, NA_HEADS))
    sin = jnp.tile(jnp.sin(ang) * sign, (1, NA_HEADS))
    return cos.astype(F32), sin.astype(F32)


def _pool_kernel(n_lat_tiles, tiles_per_seq, ctx_tiles_per_seq, cur_ref, prev_ref, next_ref,
                 w_ref, sc_ref, o_ref, ext_ref):
    j = pl.program_id(0)
    is_lat = j < n_lat_tiles
    t0 = jnp.where(is_lat, (j % tiles_per_seq) * TP, ((j - n_lat_tiles) % ctx_tiles_per_seq) * TP)
    slen = jnp.where(is_lat, tiles_per_seq * TP, ctx_tiles_per_seq * TP)
    u = cur_ref[...]
    h = POOL_HALO
    ext_ref[0:h, :] = jnp.where(t0 > 0, prev_ref[...], 0.0)
    ext_ref[h:h + TP, :] = u
    ext_ref[h + TP:h + TP + h, :] = jnp.where(t0 + TP < slen, next_ref[...], 0.0)
    t = t0 + lax.broadcasted_iota(I32, (TP, 1), 0)
    lane = lax.broadcasted_iota(I32, (TP, u.shape[1]), 1)
    grp = lane // (u.shape[1] // POOL_GROUPS)
    mean = jnp.zeros_like(u)
    for gi, w in enumerate(POOL_WINDOWS):
        acc = ext_ref[pl.ds(h - w // 2, TP), :]
        for o in range(1, w):
            acc = acc + ext_ref[pl.ds(h - w // 2 + o, TP), :]
        cnt = (jnp.minimum(t - w // 2 + w, slen) - jnp.maximum(t - w // 2, 0)).astype(F32)
        mean = jnp.where(grp == gi, acc / cnt, mean)
    y = _dot((mean - u).astype(BF16), w_ref[...]) * sc_ref[...]
    o_ref[...] = y.astype(BF16)


def pool_mix(p1, w_bd_bf, scale, n_rows, n_lat, seq_len, ctx_len):
    nt = n_rows
    hb = TP // POOL_HALO
    n_halo_blocks = p1.shape[0] // POOL_HALO
    return pl.pallas_call(
        functools.partial(_pool_kernel, n_lat // TP, seq_len // TP, ctx_len // TP),
        grid=(nt // TP,),
        in_specs=[
            pl.BlockSpec((TP, 256), lambda j: (j, 0)),
            pl.BlockSpec((POOL_HALO, 256), lambda j: (jnp.maximum(j * hb - 1, 0), 0)),
            pl.BlockSpec((POOL_HALO, 256), lambda j: (jnp.minimum((j + 1) * hb, n_halo_blocks - 1), 0)),
            pl.BlockSpec((256, 256), lambda j: (0, 0)),
            pl.BlockSpec((1, 256), lambda j: (0, 0)),
        ],
        out_specs=pl.BlockSpec((TP, 256), lambda j: (j, 0)),
        out_shape=jax.ShapeDtypeStruct((n_rows, 256), BF16),
        scratch_shapes=[pltpu.VMEM((TP + 2 * POOL_HALO, 256), F32)],
        compiler_params=_cparams(("arbitrary",)),
        name="pool_mix",
    )(p1, p1, p1, w_bd_bf, scale)


def _gmlp_kernel(p_ref, g_ref, b_ref, ws_ref, bs_ref, o_ref):
    uv = jax.nn.gelu(p_ref[...])
    w = uv.shape[1] // 2
    u = uv[:, :w]
    v = uv[:, w:]
    mu = jnp.mean(v, axis=-1, keepdims=True)
    var = jnp.mean((v - mu) ** 2, axis=-1, keepdims=True)
    v = ((v - mu) * lax.rsqrt(var + EPS) * g_ref[...] + b_ref[...]).astype(BF16)
    lane = lax.broadcasted_iota(I32, (CHUNK, w), 1)
    head = lane // (w // GMLP_HEADS)
    for c in range(TP // CHUNK):
        vc = v[c * CHUNK:(c + 1) * CHUNK]
        sv = bs_ref[...]
        for hh in range(GMLP_HEADS):
            sv = sv + jnp.where(head == hh, _dot(ws_ref[hh], vc), 0.0)
        o_ref[c * CHUNK:(c + 1) * CHUNK, :] = (u[c * CHUNK:(c + 1) * CHUNK] * sv).astype(BF16)


def gmlp_mix(p1, ln_g, ln_b, ws_bf, bs_full, n_rows):
    return pl.pallas_call(
        _gmlp_kernel,
        grid=(n_rows // TP,),
        in_specs=[
            pl.BlockSpec((TP, 512), lambda j: (j, 0)),
            pl.BlockSpec((1, 256), lambda j: (0, 0)),
            pl.BlockSpec((1, 256), lambda j: (0, 0)),
            pl.BlockSpec((GMLP_HEADS, CHUNK, CHUNK), lambda j: (0, 0, 0)),
            pl.BlockSpec((CHUNK, 256), lambda j: (0, 0)),
        ],
        out_specs=pl.BlockSpec((TP, 256), lambda j: (j, 0)),
        out_shape=jax.ShapeDtypeStruct((n_rows, 256), BF16),
        compiler_params=_cparams(("arbitrary",)),
        name="gmlp_mix",
    )(p1, ln_g, ln_b, ws_bf, bs_full)


def s5_tables(a_re, a_im, log_dt, b_re, b_im, c_re, c_im):
    hp = lax.Precision.HIGHEST
    T = SSM_T
    a_re = a_re.astype(F32)
    a_im = a_im.astype(F32)
    dt = jnp.exp(log_dt.astype(F32))[..., None]
    kk = jnp.arange(T + 1, dtype=F32)[:, None, None, None]
    mag = jnp.exp(kk * (a_re * dt)[None])
    pw_re = mag * jnp.cos(kk * (a_im * dt)[None])
    pw_im = mag * jnp.sin(kk * (a_im * dt)[None])

    def cmul(xr, xi, yr, yi):
        return xr * yr - xi * yi, xr * yi + xi * yr

    nr, ni = pw_re[1] - 1.0, pw_im[1]
    den = a_re * a_re + a_im * a_im
    zr, zi = (nr * a_re + ni * a_im) / den, (ni * a_re - nr * a_im) / den
    bb_re, bb_im = cmul(zr[..., None], zi[..., None], b_re.astype(F32), b_im.astype(F32))
    c_re = c_re.astype(F32)
    c_im = c_im.astype(F32)
    G, P = c_re.shape[1], c_re.shape[2]
    e_re, e_im = cmul(c_re[None], c_im[None], pw_re[:T, :, :, None, :], pw_im[:T, :, :, None, :])
    kern = (jnp.einsum('kdgpn,dgnq->dkgpq', e_re, bb_re, precision=hp)
            - jnp.einsum('kdgpn,dgnq->dkgpq', e_im, bb_im, precision=hp))
    s_idx = jnp.arange(T)
    lag = s_idx[None, :] - s_idx[:, None]
    sel_f = (lag[None] == s_idx[:, None, None]).astype(F32)
    sel_b = (-lag[None] == s_idx[:, None, None]).astype(F32)
    toe = (jnp.einsum('kst,kgpq->gqspt', sel_f, kern[0], precision=hp)
           + jnp.einsum('kst,kgpq->gqspt', sel_b, kern[1], precision=hp)).reshape(G, P * T, P * T)
    def pack_s(pr, pi, d):
        zr_, zi_ = cmul(pr[..., None], pi[..., None], bb_re[d][None], bb_im[d][None])
        f = lambda z: z.transpose(1, 3, 0, 2).reshape(G, P * T, -1)
        return f(zr_), f(zi_)

    sfr, sfi = pack_s(pw_re[T - 1 - s_idx, 0], pw_im[T - 1 - s_idx, 0], 0)
    sbr, sbi = pack_s(pw_re[s_idx, 1], pw_im[s_idx, 1], 1)
    sall = jnp.concatenate([sfr, sfi, sfi, sfr, sbr, sbi, sbi, sbr], axis=-1)
    def pack_w(pr, pi, d):
        zr_, zi_ = cmul(c_re[d][:, :, :, None], c_im[d][:, :, :, None],
                        pr.transpose(1, 2, 0)[:, None], pi.transpose(1, 2, 0)[:, None])
        f = lambda z: z.transpose(0, 2, 1, 3).reshape(G, -1, P * T)
        return jnp.concatenate([f(zr_), -f(zi_)], axis=1)

    wst = jnp.concatenate([pack_w(pw_re[1 + s_idx, 0], pw_im[1 + s_idx, 0], 0),
                           pack_w(pw_re[T - s_idx, 1], pw_im[T - s_idx, 1], 1)], axis=1)
    rows = []
    for dd in range(2):
        ar, ai = pw_re[T, dd], pw_im[T, dd]
        rows += [jnp.concatenate([ar, ar], -1), jnp.concatenate([-ai, ai], -1), jnp.concatenate([ai, -ai], -1)]
    rows += [jnp.zeros_like(rows[0])] * 2
    dec = jnp.stack(rows, axis=1)
    return toe.astype(BF16), sall.astype(BF16), wst.astype(BF16), dec.astype(F32)


def _s5_kernel(n_ctx_chunks, n_lat_chunks, nb, a_ref, toe_ref, sall_ref, wst_ref, dec_ref,
               o_ref, x_ref, h_ref):
    a = a_ref[0].astype(BF16)
    y = _dot(a, toe_ref[0])
    x = _dot(a, sall_ref[0])
    for i in range(4):
        x_ref[i] = x[:, i * LANES:(i + 1) * LANES]
    dec = dec_ref[0]
    n_chunks = n_ctx_chunks + n_lat_chunks
    fwd_order = list(range(n_lat_chunks, n_chunks)) + list(range(n_lat_chunks))
    bwd_order = list(range(n_chunks - 1, n_lat_chunks - 1, -1)) + list(range(n_lat_chunks - 1, -1, -1))
    for d, order in enumerate((fwd_order, bwd_order)):
        a1 = dec[3 * d:3 * d + 1]
        a2 = dec[3 * d + 1:3 * d + 2]
        a3 = dec[3 * d + 2:3 * d + 3]
        s1 = jnp.zeros((nb, LANES), F32)
        s2 = jnp.zeros((nb, LANES), F32)
        for c in order:
            if c < n_lat_chunks:
                r = pl.ds(c, nb, stride=n_lat_chunks)
            else:
                r = pl.ds(nb * n_lat_chunks + c - n_lat_chunks, nb, stride=n_ctx_chunks)
            h_ref[d, r, :] = s1
            x1 = x_ref[2 * d, r, :]
            x2 = x_ref[2 * d + 1, r, :]
            s1, s2 = a1 * s1 + a2 * s2 + x1, a1 * s2 + a3 * s1 + x2
    for d in range(2):
        y = y + _dot(h_ref[d].astype(BF16), wst_ref[0, d * LANES:(d + 1) * LANES, :])
    o_ref[0] = y


def s5_core(a_g, toe, sall, wst, dec, n_ctx_chunks, n_lat_chunks, nb):
    G, rows, w = a_g.shape
    return pl.pallas_call(
        functools.partial(_s5_kernel, n_ctx_chunks, n_lat_chunks, nb),
        grid=(G,),
        in_specs=[
            pl.BlockSpec((1, rows, w), lambda g: (g, 0, 0)),
            pl.BlockSpec((1, w, w), lambda g: (g, 0, 0)),
            pl.BlockSpec((1, w, 512), lambda g: (g, 0, 0)),
            pl.BlockSpec((1, 256, w), lambda g: (g, 0, 0)),
            pl.BlockSpec((1, 8, LANES), lambda g: (g, 0, 0)),
        ],
        out_specs=pl.BlockSpec((1, rows, w), lambda g: (g, 0, 0)),
        out_shape=jax.ShapeDtypeStruct((G, rows, w), F32),
        scratch_shapes=[pltpu.VMEM((4, rows, LANES), F32), pltpu.VMEM((2, rows, LANES), F32)],
        compiler_params=_cparams(("arbitrary",)),
        name="s5_core",
    )(a_g, toe, sall, wst, dec)


def s5_to_groups(u):
    T = SSM_T
    G = u.shape[1] // SSM_CH
    return u.reshape(-1, T, G, SSM_CH).transpose(2, 0, 3, 1).reshape(G, -1, SSM_CH * T)


def s5_from_groups(y):
    T = SSM_T
    G = y.shape[0]
    return y.reshape(G, -1, SSM_CH, T).transpose(1, 3, 0, 2).reshape(-1, G * SSM_CH)


def _s5_post_kernel(y_ref, u_ref, d_ref, w_ref, o_ref):
    y = jax.nn.gelu(y_ref[...] + d_ref[...] * u_ref[...])
    z = _dot(y.astype(BF16), w_ref[...])
    w = z.shape[1] // 2
    o_ref[...] = (z[:, :w] * jax.nn.sigmoid(z[:, w:])).astype(BF16)


def s5_post(y_nat, pu, d, w_glu_bf, n_rows):
    nt = n_rows
    return pl.pallas_call(
        _s5_post_kernel,
        grid=(nt // TM,),
        in_specs=[
            pl.BlockSpec((TM, 256), lambda j: (j, 0)),
            pl.BlockSpec((TM, 256), lambda j: (j, 0)),
            pl.BlockSpec((1, 256), lambda j: (0, 0)),
            pl.BlockSpec((256, 512), lambda j: (0, 0)),
        ],
        out_specs=pl.BlockSpec((TM, 256), lambda j: (j, 0)),
        out_shape=jax.ShapeDtypeStruct((nt, 256), BF16),
        compiler_params=_cparams(("arbitrary",)),
        name="s5_post",
    )(y_nat, pu, d, w_glu_bf)


def na_bias_table(rpb):
    ci = jnp.arange(GRID_W)
    cs = jnp.clip(ci - WIN_COLS // 2, 0, GRID_W - WIN_COLS)
    kc = jnp.arange(GRID_W)
    inside = (kc[None, :] >= cs[:, None]) & (kc[None, :] < cs[:, None] + WIN_COLS)
    col_off = jnp.clip(kc[None, :] - ci[:, None] + (WIN_COLS - 1), 0, 2 * WIN_COLS - 2)
    ty = jnp.arange(WIN_ROWS)
    jj = jnp.arange(WIN_ROWS)
    row_off = jj[None, :] - ty[:, None] + (WIN_ROWS - 1)
    tab = rpb.astype(F32)[:, row_off[:, :, None, None], col_off[None, None, :, :]]
    tab = jnp.where(inside[None, None, None], tab, NEG)
    tab = tab.transpose(1, 0, 3, 2, 4)
    return tab.reshape(WIN_ROWS, NA_HEADS * GRID_W, WIN_ROWS * GRID_W)


def _heads_stack(q):
    lane = lax.broadcasted_iota(I32, q.shape, 1)
    zero = jnp.zeros_like(q)
    return jnp.concatenate([jnp.where(lane // HEAD_DIM == hh, q, zero) for hh in range(NA_HEADS)], axis=0)


def _heads_merge(o4, m):
    lane = lax.broadcasted_iota(I32, (m, o4.shape[1]), 1)
    out = jnp.zeros((m, o4.shape[1]), F32)
    for hh in range(NA_HEADS):
        out = jnp.where(lane // HEAD_DIM == hh, o4[hh * m:(hh + 1) * m], out)
    return out


def _na_kernel(rows_per_step, n_rows, lat_steps, q_ref, k_ref, v_ref, kc_ref, vc_ref, tab_ref, o_ref):
    i = pl.program_id(1)
    kc = kc_ref[...]
    vc = vc_ref[...]
    nkeys = WIN_ROWS * GRID_W

    @pl.when(i < lat_steps)
    def _():
        def body(a, carry):
            r = i * rows_per_step + a
            rs = jnp.clip(r - WIN_ROWS // 2, 0, n_rows - WIN_ROWS)
            ty = r - rs
            q0 = pl.multiple_of(a * GRID_W, GRID_W)
            k0 = pl.multiple_of(rs * GRID_W, GRID_W)
            q4 = _heads_stack(q_ref[pl.ds(q0, GRID_W), :])
            kw = k_ref[pl.ds(k0, nkeys), :]
            vw = v_ref[pl.ds(k0, nkeys), :]
            s = _dot_nt(q4, kw) + tab_ref[ty]
            sc = _dot_nt(q4, kc)
            m = jnp.maximum(jnp.max(s, axis=-1, keepdims=True), jnp.max(sc, axis=-1, keepdims=True))
            e = jnp.exp(s - m)
            ec = jnp.exp(sc - m)
            den = jnp.sum(e, axis=-1, keepdims=True) + jnp.sum(ec, axis=-1, keepdims=True)
            o4 = (_dot(e.astype(BF16), vw) + _dot(ec.astype(BF16), vc)) / den
            o_ref[pl.ds(q0, GRID_W), :] = _heads_merge(o4, GRID_W).astype(BF16)
            return carry

        lax.fori_loop(0, rows_per_step, body, 0)

    @pl.when(i >= lat_steps)
    def _():
        q4 = _heads_stack(q_ref[...])
        s = _dot_nt(q4, kc)
        m = jnp.max(s, axis=-1, keepdims=True)
        e = jnp.exp(s - m)
        den = jnp.sum(e, axis=-1, keepdims=True)
        o4 = _dot(e.astype(BF16), vc) / den
        o_ref[...] = _heads_merge(o4, q_ref.shape[0]).astype(BF16)


def na_attention(q, k, v, tab, nb, seq_len, ctx_len, with_ctx):
    tq = ctx_len
    rows_per_step = tq // GRID_W
    n_rows = seq_len // GRID_W
    lat_steps = seq_len // tq
    ctx_blk0 = nb * seq_len // ctx_len

    def q_blk(b, i):
        return (jnp.where(i < lat_steps, b * lat_steps + i, ctx_blk0 + b), 0)

    n_out = nb * seq_len + (nb * ctx_len if with_ctx else 0)
    return pl.pallas_call(
        functools.partial(_na_kernel, rows_per_step, n_rows, lat_steps),
        grid=(nb, lat_steps + (1 if with_ctx else 0)),
        in_specs=[
            pl.BlockSpec((tq, 256), q_blk),
            pl.BlockSpec((seq_len, 256), lambda b, i: (b, 0)),
            pl.BlockSpec((seq_len, 256), lambda b, i: (b, 0)),
            pl.BlockSpec((ctx_len, 256), lambda b, i: (ctx_blk0 + b, 0)),
            pl.BlockSpec((ctx_len, 256), lambda b, i: (ctx_blk0 + b, 0)),
            pl.BlockSpec(tab.shape, lambda b, i: (0, 0, 0)),
        ],
        out_specs=pl.BlockSpec((tq, 256), q_blk),
        out_shape=jax.ShapeDtypeStruct((n_out, 256), BF16),
        compiler_params=_cparams(("arbitrary", "arbitrary")),
        name="na_attention",
    )(q, k, v, k, v, tab)


def _merge_kernel(tiles_per_seq, n_lat_tiles, x_ref, mod_ref, gpre_ref, gpost_ref, gffn_ref,
                  ya_ref, yb_ref, yg_ref, yd_ref, wbg_ref, bbg_ref, wbr_ref, wout_ref,
                  wrh_ref, wrl_ref, br_ref, tri_ref, xn_ref, hf_ref, ti_ref, rk_ref, tw_ref, cnt_ref,
                  run_ref):
    x = x_ref[...]
    d = x.shape[1]
    mod = mod_ref[0]
    h = _rms(x, gpre_ref[...]) * (1.0 + mod[1:2]) + mod[0:1]
    hb = h.astype(BF16)
    acc = jnp.zeros_like(x)
    for bi, br_ref_i in enumerate((ya_ref, yb_ref, yg_ref, yd_ref)):
        gate = jax.nn.sigmoid(_dot(hb, wbg_ref[:, bi * d:(bi + 1) * d]) + bbg_ref[:, bi * d:(bi + 1) * d])
        acc = acc + gate * _dot(br_ref_i[...], wbr_ref[bi])
    y = _dot(acc.astype(BF16), wout_ref[...])
    xn = x + mod[2:3] * _rms(y, gpost_ref[...])
    xn_ref[...] = xn
    hf = _rms(xn, gffn_ref[...]) * (1.0 + mod[4:5]) + mod[3:4]
    hf_hi = hf.astype(BF16)
    bits = lax.bitcast_convert_type(hf_hi.astype(F32), jnp.uint32)
    hf_ref[...] = (bits[:, :d // 2] >> 16) | (bits[:, d // 2:] & jnp.uint32(0xFFFF0000))
    hf_lo = (hf - hf_hi.astype(F32)).astype(BF16)
    logits = _dot(hf_hi, wrh_ref[...]) + (_dot(hf_hi, wrl_ref[...]) + _dot(hf_lo, wrh_ref[...])) + br_ref[...]

    @pl.when(pl.program_id(0) == 0)
    def _():
        run_ref[...] = jnp.zeros_like(run_ref)

    ne = run_ref.shape[0]
    tm = logits.shape[0]
    work = jnp.transpose(logits)[:ne]
    erow = lax.broadcasted_iota(I32, work.shape, 0)
    idxs, es, ohs = [], [], []
    v0 = None
    for kk in range(TOP_K):
        m = jnp.max(work, axis=0, keepdims=True)
        idx = jnp.min(jnp.where(work == m, erow, ne), axis=0, keepdims=True)
        oh = erow == idx
        if kk == 0:
            v0 = m
            es.append(jnp.ones_like(m))
        else:
            es.append(jnp.exp(m - v0))
        idxs.append(idx)
        ohs.append(oh)
        work = jnp.where(oh, -jnp.inf, work)
    den = es[0] + es[1] + es[2] + es[3]
    ohf = [o.astype(F32) for o in ohs]
    cnt = _dot(jnp.concatenate([o.astype(BF16) for o in ohf], axis=0), tri_ref[...])
    base = run_ref[:, 0:1]
    r8 = lax.broadcasted_iota(I32, (8, tm), 0)
    r128 = lax.broadcasted_iota(I32, (LANES, tm), 0)
    ti8 = jnp.zeros((8, tm), I32)
    rk8 = jnp.zeros((8, tm), I32)
    twt = jnp.zeros((LANES, tm), F32)
    for kk in range(TOP_K):
        rank = jnp.sum(ohf[kk] * (base + cnt[kk * ne:(kk + 1) * ne]), axis=0, keepdims=True)
        base = base + jnp.sum(ohf[kk], axis=1, keepdims=True)
        ti8 = jnp.where(r8 == kk, idxs[kk], ti8)
        rk8 = jnp.where(r8 == kk, rank.astype(I32), rk8)
        twt = jnp.where(r128 == kk, es[kk] / den, twt)
    run_ref[...] = jnp.broadcast_to(base, run_ref.shape)
    ti_ref[...] = ti8
    rk_ref[...] = rk8
    tw_ref[...] = jnp.transpose(twt)
    cnt_ref[...] = run_ref[...]


def merge_and_route(x, mod, gpre, gpost, gffn, ya, yb, yg, yd, wbg, bbg, wbr, wout, wrh, wrl, br,
                    n_tokens, n_lat, seq_len, n_experts):
    d = x.shape[1]
    n_lat_tiles = n_lat // TM
    tiles_per_seq = seq_len // TM

    def mod_row(j):
        return jnp.where(j < n_lat_tiles, j // tiles_per_seq, 4)

    tri = jnp.triu(jnp.ones((TM, TM), F32), 1).astype(BF16)
    tok = lambda w: pl.BlockSpec((TM, w), lambda j: (j, 0))
    lanes = pl.BlockSpec((8, TM), lambda j: (0, j))
    full = lambda shape: pl.BlockSpec(shape, lambda j: tuple(0 for _ in shape))
    return pl.pallas_call(
        functools.partial(_merge_kernel, tiles_per_seq, n_lat_tiles),
        grid=(n_tokens // TM,),
        in_specs=[
            tok(d),
            pl.BlockSpec((1, 6, d), lambda j: (mod_row(j), 0, 0)),
            full((1, d)), full((1, d)), full((1, d)),
            tok(256), tok(256), tok(256), tok(256),
            full(wbg.shape), full(bbg.shape), full(wbr.shape), full(wout.shape),
            full(wrh.shape), full(wrl.shape), full(br.shape), full(tri.shape),
        ],
        out_specs=[tok(d), tok(d // 2), lanes, lanes, tok(LANES), full((n_experts, LANES))],
        out_shape=[
            jax.ShapeDtypeStruct((n_tokens, d), F32),
            jax.ShapeDtypeStruct((n_tokens, d // 2), jnp.uint32),
            jax.ShapeDtypeStruct((8, n_tokens), I32),
            jax.ShapeDtypeStruct((8, n_tokens), I32),
            jax.ShapeDtypeStruct((n_tokens, LANES), F32),
            jax.ShapeDtypeStruct((n_experts, LANES), F32),
        ],
        scratch_shapes=[pltpu.VMEM((n_experts, LANES), F32)],
        compiler_params=_cparams(("arbitrary",)),
        name="merge_and_route",
    )(x, mod, gpre, gpost, gffn, ya, yb, yg, yd, wbg, bbg, wbr, wout, wrh, wrl, br, tri)


def _expert_kernel(be_ref, nv_ref, x_ref, w1_ref, b1_ref, w2_ref, b2_ref, o_ref, w1b_ref, w2b_ref):
    i = pl.program_id(0)
    prev = be_ref[jnp.maximum(i - 1, 0)]
    changed = jnp.logical_or(i == 0, be_ref[i] != prev)

    @pl.when(jnp.logical_and(i < nv_ref[0], changed))
    def _():
        w1b_ref[...] = w1_ref[0].astype(BF16)
        w2b_ref[...] = w2_ref[0].astype(BF16)

    @pl.when(i < nv_ref[0])
    def _():
        w = x_ref[...]
        half = w.shape[1]
        x_lo = lax.bitcast_convert_type(w << 16, F32).astype(BF16)
        x_hi = lax.bitcast_convert_type(w & jnp.uint32(0xFFFF0000), F32).astype(BF16)
        h = _dot(x_lo, w1b_ref[0:half, :]) + _dot(x_hi, w1b_ref[half:, :]) + b1_ref[0]
        de = h.shape[1] // 2
        g = jnp.minimum(h[:, :de], SWIGLU_LIMIT)
        lin = jnp.clip(h[:, de:], -SWIGLU_LIMIT, SWIGLU_LIMIT)
        act = g * jax.nn.sigmoid(SWIGLU_ALPHA * g) * (lin + 1.0)
        o_ref[...] = _dot(act.astype(BF16), w2b_ref[...]) + b2_ref[0]

    @pl.when(i >= nv_ref[0])
    def _():
        o_ref[...] = jnp.zeros_like(o_ref)


def moe_experts(xb, block_e, n_valid, w1, b1, w2, b2):
    p = xb.shape[0]
    ne, d, de2 = w1.shape
    n_blocks = p // MOE_BM
    grid_spec = pltpu.PrefetchScalarGridSpec(
        num_scalar_prefetch=2,
        grid=(n_blocks,),
        in_specs=[
            pl.BlockSpec((MOE_BM, d // 2), lambda i, be, nv: (i, 0)),
            pl.BlockSpec((1, d, de2), lambda i, be, nv: (be[i], 0, 0)),
            pl.BlockSpec((1, 1, de2), lambda i, be, nv: (be[i], 0, 0)),
            pl.BlockSpec((1, de2 // 2, d), lambda i, be, nv: (be[i], 0, 0)),
            pl.BlockSpec((1, 1, d), lambda i, be, nv: (be[i], 0, 0)),
        ],
        out_specs=pl.BlockSpec((MOE_BM, d), lambda i, be, nv: (i, 0)),
        scratch_shapes=[pltpu.VMEM((d, de2), BF16), pltpu.VMEM((de2 // 2, d), BF16)],
    )
    return pl.pallas_call(
        _expert_kernel,
        grid_spec=grid_spec,
        out_shape=jax.ShapeDtypeStruct((p, d), F32),
        compiler_params=_cparams(("arbitrary",)),
        name="moe_experts",
    )(block_e, n_valid, xb, w1, b1.reshape(ne, 1, de2), w2, b2.reshape(ne, 1, d))


def _combine_kernel(dcur_ref, dnext_ref, x_ref, mod_ref, g_ref, tw_ref, yblk_ref, o_ref, gbuf, sem):
    j = pl.program_id(0)
    n = pl.num_programs(0)
    slot = j % 2
    tm = x_ref.shape[0]

    def row_copy(dref, s, t, kk):
        return pltpu.make_async_copy(yblk_ref.at[pl.ds(dref[kk, t], 1)], gbuf.at[s, kk, pl.ds(t, 1)], sem.at[s])

    def issue(dref, s):
        def body(t, carry):
            for kk in range(TOP_K):
                row_copy(dref, s, t, kk).start()
            return carry
        lax.fori_loop(0, tm, body, 0)

    @pl.when(j == 0)
    def _():
        issue(dcur_ref, slot)

    @pl.when(j + 1 < n)
    def _():
        issue(dnext_ref, 1 - slot)

    def wait(t, carry):
        for kk in range(TOP_K):
            row_copy(dcur_ref, slot, t, kk).wait()
        return carry
    lax.fori_loop(0, tm, wait, 0)

    tw = tw_ref[...]
    y = tw[:, 0:1] * gbuf[slot, 0]
    for kk in range(1, TOP_K):
        y = y + tw[:, kk:kk + 1] * gbuf[slot, kk]
    o_ref[...] = x_ref[...] + mod_ref[0, 5:6, :] * _rms(y, g_ref[...])


def moe_combine(dest_t, xn, mod, gain, tw, yblk, n_tokens, n_lat, seq_len):
    d = xn.shape[1]
    n_lat_tiles = n_lat // TM
    tiles_per_seq = seq_len // TM
    n_tiles = n_tokens // TM

    def mod_row(j):
        return jnp.where(j < n_lat_tiles, j // tiles_per_seq, 4)

    return pl.pallas_call(
        _combine_kernel,
        grid=(n_tiles,),
        in_specs=[
            pl.BlockSpec((8, TM), lambda j: (0, j), memory_space=pltpu.SMEM),
            pl.BlockSpec((8, TM), lambda j: (0, jnp.minimum(j + 1, n_tiles - 1)), memory_space=pltpu.SMEM),
            pl.BlockSpec((TM, d), lambda j: (j, 0)),
            pl.BlockSpec((1, 6, d), lambda j: (mod_row(j), 0, 0)),
            pl.BlockSpec((1, d), lambda j: (0, 0)),
            pl.BlockSpec((TM, LANES), lambda j: (j, 0)),
            pl.BlockSpec(memory_space=pl.ANY),
        ],
        out_specs=pl.BlockSpec((TM, d), lambda j: (j, 0)),
        out_shape=jax.ShapeDtypeStruct((n_tokens, d), F32),
        scratch_shapes=[pltpu.VMEM((2, TOP_K, TM, d), F32), pltpu.SemaphoreType.DMA((2,))],
        compiler_params=_cparams(("arbitrary",)),
        name="moe_combine",
    )(dest_t, dest_t, xn, mod, gain, tw, yblk)


def _dest_kernel(n_experts, ps_ref, ti_ref, rk_ref, o_ref):
    ti = ti_ref[...]
    dest = rk_ref[...]
    for e in range(n_experts):
        dest = dest + jnp.where(ti == e, ps_ref[e], 0)
    o_ref[...] = dest


def moe_dest(pstart, ti_t, rk_t):
    n_tokens = ti_t.shape[1]
    lanes = pl.BlockSpec((8, TM), lambda j, ps: (0, j))
    return pl.pallas_call(
        functools.partial(_dest_kernel, pstart.shape[0]),
        grid_spec=pltpu.PrefetchScalarGridSpec(
            num_scalar_prefetch=1, grid=(n_tokens // TM,), in_specs=[lanes, lanes], out_specs=lanes),
        out_shape=jax.ShapeDtypeStruct(ti_t.shape, I32),
        compiler_params=_cparams(("arbitrary",)),
        name="moe_dest",
    )(pstart, ti_t, rk_t)


def _dispatch_kernel(dest_ref, hf_ref, xb_in_ref, xb_ref, sem):
    del xb_in_ref
    tm = hf_ref.shape[0]

    def row_copy(t, kk):
        return pltpu.make_async_copy(hf_ref.at[pl.ds(t, 1)], xb_ref.at[pl.ds(dest_ref[kk, t], 1)], sem)

    def start(t, carry):
        for kk in range(TOP_K):
            row_copy(t, kk).start()
        return carry
    lax.fori_loop(0, tm, start, 0)

    def wait(t, carry):
        for kk in range(TOP_K):
            row_copy(t, kk).wait()
        return carry
    lax.fori_loop(0, tm, wait, 0)


def moe_dispatch(dest_t, hfp, n_rows):
    n_tokens, w = hfp.shape
    xb0 = jnp.zeros((n_rows, w), hfp.dtype)
    return pl.pallas_call(
        _dispatch_kernel,
        grid=(n_tokens // TM,),
        in_specs=[
            pl.BlockSpec((8, TM), lambda j: (0, j), memory_space=pltpu.SMEM),
            pl.BlockSpec((TM, w), lambda j: (j, 0)),
            pl.BlockSpec(memory_space=pl.ANY),
        ],
        out_specs=pl.BlockSpec(memory_space=pl.ANY),
        out_shape=jax.ShapeDtypeStruct((n_rows, w), hfp.dtype),
        input_output_aliases={2: 0},
        scratch_shapes=[pltpu.SemaphoreType.DMA(())],
        compiler_params=_cparams(("arbitrary",)),
        name="moe_dispatch",
    )(dest_t, hfp, xb0)


def kernel(x, c, ctx, c_ctx, ada_w, ada_b, norm_mix_pre, norm_mix_post, norm_ffn_pre, norm_ffn_post,
           w_in, w_branch_gate, b_branch_gate, w_branch, w_out, pool_w, pool_scale,
           ssm_a_re, ssm_a_im, ssm_log_dt, ssm_b_re, ssm_b_im, ssm_c_re, ssm_c_im, ssm_d, ssm_w_glu,
           gmlp_ln_g, gmlp_ln_b, gmlp_w_s, gmlp_b_s, na_rpb,
           router_w, router_b, expert_w1, expert_b1, expert_w2, expert_b2):
    nb, seq_len, d = x.shape
    ctx_len = ctx.shape[1]
    depth = ada_w.shape[0]
    n_lat = nb * seq_len
    n_all = n_lat + nb * ctx_len
    n_experts = router_w.shape[-1]
    assert nb == 4 and seq_len % TM == 0 and (nb * ctx_len) % TM == 0 and ctx_len % TP == 0

    xs = jnp.concatenate([x.reshape(n_lat, d), ctx.reshape(nb * ctx_len, d)], axis=0)
    cvec = jnp.concatenate([c, c_ctx[None, :], jnp.zeros((3, d), F32)], axis=0)
    mods = ada_modulation(cvec, ada_w, ada_b).reshape(depth, 8, 6, d)
    cos_t, sin_t = rope_tables(seq_len)

    out = None
    for l in range(depth):
        last = l == depth - 1
        mod = mods[l]
        row = lambda v: v.reshape(1, -1)
        pa, pu, pg, q, k, v = premix(xs, mod, row(norm_mix_pre[l]), w_in[l].astype(BF16), cos_t, sin_t,
                                     n_lat, seq_len)

        n_mix = n_lat if last else n_all
        w_bd = jax.scipy.linalg.block_diag(*[pool_w[l, g] for g in range(POOL_GROUPS)]).astype(BF16)
        ya = pool_mix(pa, w_bd, row(pool_scale[l]), n_mix, n_lat, seq_len, ctx_len)

        bs_full = jnp.repeat(gmlp_b_s[l].T, 256 // GMLP_HEADS, axis=1)
        yg = gmlp_mix(pg, row(gmlp_ln_g[l]), row(gmlp_ln_b[l]), gmlp_w_s[l].astype(BF16), bs_full, n_mix)

        toe, sall, wst, dec = s5_tables(ssm_a_re[l], ssm_a_im[l], ssm_log_dt[l], ssm_b_re[l], ssm_b_im[l],
                                        ssm_c_re[l], ssm_c_im[l])
        y_g = s5_core(s5_to_groups(pu), toe, sall, wst, dec, ctx_len // SSM_T, seq_len // SSM_T, nb)
        yb = s5_post(s5_from_groups(y_g), pu, row(ssm_d[l]), ssm_w_glu[l].astype(BF16), n_mix)

        yd = na_attention(q, k, v, na_bias_table(na_rpb[l]), nb, seq_len, ctx_len, not last)

        rw = jnp.pad(router_w[l], ((0, 0), (0, LANES - n_experts)))
        rw_hi = rw.astype(BF16)
        rw_lo = (rw - rw_hi.astype(F32)).astype(BF16)
        rb = jnp.pad(router_b[l], (0, LANES - n_experts), constant_values=NEG).reshape(1, LANES)
        n_tok = n_lat if last else n_all
        xn, hfp, ti_t, rk_t, tw, counts = merge_and_route(
            xs, mod, row(norm_mix_pre[l]), row(norm_mix_post[l]), row(norm_ffn_pre[l]),
            ya, yb, yg, yd, w_branch_gate[l].astype(BF16), row(b_branch_gate[l]),
            w_branch[l].astype(BF16), w_out[l].astype(BF16), rw_hi, rw_lo, rb, n_tok, n_lat, seq_len, n_experts)

        cnt = counts[:, 0].astype(I32)
        padded = (cnt + MOE_BM - 1) // MOE_BM * MOE_BM
        pend = jnp.cumsum(padded)
        n_blocks = n_tok * TOP_K // MOE_BM + n_experts
        blk_start = jnp.arange(n_blocks, dtype=I32) * MOE_BM
        block_e = jnp.minimum(jnp.sum((pend[None, :] <= blk_start[:, None]).astype(I32), axis=1), n_experts - 1)
        n_valid = (pend[-1] // MOE_BM).astype(I32).reshape(1)

        dest_t = moe_dest(pend - padded, ti_t, rk_t)
        xb = moe_dispatch(dest_t, hfp, n_blocks * MOE_BM)
        yblk = moe_experts(xb, block_e, n_valid, expert_w1[l], expert_b1[l], expert_w2[l], expert_b2[l])
        xs = moe_combine(dest_t, xn, mod, row(norm_ffn_post[l]), tw, yblk, n_tok, n_lat, seq_len)
        out = xs
    return out.reshape(nb, seq_len, d)
```

```python
import functools
import math

import jax
import jax.numpy as jnp
import jax.scipy.linalg
from jax import lax
from jax.experimental import pallas as pl
from jax.experimental.pallas import tpu as pltpu

F32 = jnp.float32
BF16 = jnp.bfloat16
I32 = jnp.int32

GRID_W = 64
EPS = 1e-6
POOL_GROUPS = 4
POOL_WINDOWS = (2, 4, 8, 16)
SSM_CH = 16
SSM_N = 64
GMLP_HEADS = 4
CHUNK = 128
NA_HEADS = 4
HEAD_DIM = 64
WIN_ROWS = 8
WIN_COLS = 16
ROPE_BASE = 10000.0
BRANCH_W = 256
N_BRANCH = 4
TOP_K = 4
SWIGLU_LIMIT = 7.0
SWIGLU_ALPHA = 1.702

TM = 512
TP = 256
POOL_HALO = 16
SSM_T = 32
MOE_BM = 256
DMA_UNROLL = 8
LANES = 128
NEG = -1e30
VMEM_LIMIT = 56 * 1024 * 1024


def _cparams(sem):
    return pltpu.CompilerParams(dimension_semantics=sem, vmem_limit_bytes=VMEM_LIMIT)


def _dot(a, b):
    return jnp.dot(a, b, preferred_element_type=F32)


def _dot_nt(a, b):
    return lax.dot_general(a, b, (((1,), (1,)), ((), ())), preferred_element_type=F32)


def _ada_kernel(c_ref, w_ref, b_ref, o_ref):
    c = c_ref[...]
    s = c * jax.nn.sigmoid(c)
    o_ref[0] = jnp.dot(s, w_ref[0], preferred_element_type=F32,
                       precision=lax.Precision.HIGHEST) + b_ref[0]


def ada_modulation(cvec, ada_w, ada_b):
    depth, d, n = ada_w.shape
    tn = 1536
    return pl.pallas_call(
        _ada_kernel,
        grid=(depth, n // tn),
        in_specs=[
            pl.BlockSpec((8, d), lambda l, j: (0, 0)),
            pl.BlockSpec((1, d, tn), lambda l, j: (l, 0, j)),
            pl.BlockSpec((1, 1, tn), lambda l, j: (l, 0, j)),
        ],
        out_specs=pl.BlockSpec((1, 8, tn), lambda l, j: (l, 0, j)),
        out_shape=jax.ShapeDtypeStruct((depth, 8, n), F32),
        compiler_params=_cparams(("arbitrary", "arbitrary")),
        name="ada_modulation",
    )(cvec, ada_w, ada_b.reshape(depth, 1, n))


def _rms(x, g):
    return x * lax.rsqrt(jnp.mean(x * x, axis=-1, keepdims=True) + EPS) * g


def _premix_kernel(n_lat_tiles, x_ref, mod_ref, g_ref, w_ref, cos_ref, sin_ref,
                   pa_ref, pu_ref, pg_ref, q_ref, k_ref, v_ref):
    j = pl.program_id(0)
    x = x_ref[...]
    h = _rms(x, g_ref[...])
    h = h * (1.0 + mod_ref[0, 1:2, :]) + mod_ref[0, 0:1, :]
    p = _dot(h.astype(BF16), w_ref[...])
    wa = p.shape[1] - 768
    pa_ref[...] = p[:, 0:256]
    pu_ref[...] = p[:, 256:512]
    pg_ref[...] = p[:, 512:wa]
    q = p[:, wa:wa + 256]
    k = p[:, wa + 256:wa + 512]
    v = p[:, wa + 512:wa + 768]
    is_lat = j < n_lat_tiles
    cos = jnp.where(is_lat, cos_ref[...], 1.0)
    sin = jnp.where(is_lat, sin_ref[...], 0.0)
    lane = lax.broadcasted_iota(I32, q.shape, 1)
    first = (lane % 32) < 16

    def rope(t):
        partner = jnp.where(first, pltpu.roll(t, 256 - 16, axis=1), pltpu.roll(t, 16, axis=1))
        return t * cos + partner * sin

    q_ref[...] = (rope(q) * (HEAD_DIM ** -0.5)).astype(BF16)
    k_ref[...] = rope(k).astype(BF16)
    v_ref[...] = v.astype(BF16)


def premix(x, mod, gain, w_in_bf, cos_t, sin_t, n_lat, seq_len):
    nt, d = x.shape
    in_w = w_in_bf.shape[1]
    wa = in_w - 768
    n_lat_tiles = n_lat // TM
    tiles_per_seq = seq_len // TM

    def mod_row(j):
        return jnp.where(j < n_lat_tiles, j // tiles_per_seq, 4)

    def tab_blk(j):
        return jnp.where(j < n_lat_tiles, j % tiles_per_seq, 0)

    return pl.pallas_call(
        functools.partial(_premix_kernel, n_lat_tiles),
        grid=(nt // TM,),
        in_specs=[
            pl.BlockSpec((TM, d), lambda j: (j, 0)),
            pl.BlockSpec((1, 6, d), lambda j: (mod_row(j), 0, 0)),
            pl.BlockSpec((1, d), lambda j: (0, 0)),
            pl.BlockSpec((d, in_w), lambda j: (0, 0)),
            pl.BlockSpec((TM, 256), lambda j: (tab_blk(j), 0)),
            pl.BlockSpec((TM, 256), lambda j: (tab_blk(j), 0)),
        ],
        out_specs=[
            pl.BlockSpec((TM, 256), lambda j: (j, 0)),
            pl.BlockSpec((TM, 256), lambda j: (j, 0)),
            pl.BlockSpec((TM, wa - 512), lambda j: (j, 0)),
            pl.BlockSpec((TM, 256), lambda j: (j, 0)),
            pl.BlockSpec((TM, 256), lambda j: (j, 0)),
            pl.BlockSpec((TM, 256), lambda j: (j, 0)),
        ],
        out_shape=[
            jax.ShapeDtypeStruct((nt, 256), F32),
            jax.ShapeDtypeStruct((nt, 256), F32),
            jax.ShapeDtypeStruct((nt, wa - 512), F32),
            jax.ShapeDtypeStruct((nt, 256), BF16),
            jax.ShapeDtypeStruct((nt, 256), BF16),
            jax.ShapeDtypeStruct((nt, 256), BF16),
        ],
        compiler_params=_cparams(("arbitrary",)),
        name="premix",
    )(x, mod, gain, w_in_bf, cos_t, sin_t)


def rope_tables(seq_len):
    t = jnp.arange(seq_len)
    half = HEAD_DIM // 2
    nf = half // 2
    inv = ROPE_BASE ** (-jnp.arange(nf, dtype=F32) / nf)
    d = jnp.arange(HEAD_DIM)
    pos = jnp.where((d // half)[None, :] == 0, (t // GRID_W)[:, None], (t % GRID_W)[:, None]).astype(F32)
    ang = pos * inv[(d % half) % nf][None, :]
    sign = jnp.where((d % half) < nf, -1.0, 1.0)[None, :]
    cos = jnp.tile(jnp.cos(ang), (1, NA_HEADS))
    sin = jnp.tile(jnp.sin(ang) * sign, (1, NA_HEADS))
    return cos.astype(F32), sin.astype(F32)


def _pool_kernel(n_lat_tiles, tiles_per_seq, ctx_tiles_per_seq, cur_ref, prev_ref, next_ref,
                 w_ref, sc_ref, o_ref, ext_ref):
    j = pl.program_id(0)
    is_lat = j < n_lat_tiles
    t0 = jnp.where(is_lat, (j % tiles_per_seq) * TP, ((j - n_lat_tiles) % ctx_tiles_per_seq) * TP)
    slen = jnp.where(is_lat, tiles_per_seq * TP, ctx_tiles_per_seq * TP)
    u = cur_ref[...]
    h = POOL_HALO
    ext_ref[0:h, :] = jnp.where(t0 > 0, prev_ref[...], 0.0)
    ext_ref[h:h + TP, :] = u
    ext_ref[h + TP:h + TP + h, :] = jnp.where(t0 + TP < slen, next_ref[...], 0.0)
    t = t0 + lax.broadcasted_iota(I32, (TP, 1), 0)
    lane = lax.broadcasted_iota(I32, (TP, u.shape[1]), 1)
    grp = lane // (u.shape[1] // POOL_GROUPS)
    mean = jnp.zeros_like(u)
    for gi, w in enumerate(POOL_WINDOWS):
        acc = ext_ref[pl.ds(h - w // 2, TP), :]
        for o in range(1, w):
            acc = acc + ext_ref[pl.ds(h - w // 2 + o, TP), :]
        cnt = (jnp.minimum(t - w // 2 + w, slen) - jnp.maximum(t - w // 2, 0)).astype(F32)
        mean = jnp.where(grp == gi, acc / cnt, mean)
    y = _dot((mean - u).astype(BF16), w_ref[...]) * sc_ref[...]
    o_ref[...] = y.astype(BF16)


def pool_mix(p1, w_bd_bf, scale, n_rows, n_lat, seq_len, ctx_len):
    nt = n_rows
    hb = TP // POOL_HALO
    n_halo_blocks = p1.shape[0] // POOL_HALO
    return pl.pallas_call(
        functools.partial(_pool_kernel, n_lat // TP, seq_len // TP, ctx_len // TP),
        grid=(nt // TP,),
        in_specs=[
            pl.BlockSpec((TP, 256), lambda j: (j, 0)),
            pl.BlockSpec((POOL_HALO, 256), lambda j: (jnp.maximum(j * hb - 1, 0), 0)),
            pl.BlockSpec((POOL_HALO, 256), lambda j: (jnp.minimum((j + 1) * hb, n_halo_blocks - 1), 0)),
            pl.BlockSpec((256, 256), lambda j: (0, 0)),
            pl.BlockSpec((1, 256), lambda j: (0, 0)),
        ],
        out_specs=pl.BlockSpec((TP, 256), lambda j: (j, 0)),
        out_shape=jax.ShapeDtypeStruct((n_rows, 256), BF16),
        scratch_shapes=[pltpu.VMEM((TP + 2 * POOL_HALO, 256), F32)],
        compiler_params=_cparams(("arbitrary",)),
        name="pool_mix",
    )(p1, p1, p1, w_bd_bf, scale)


def _gmlp_kernel(p_ref, g_ref, b_ref, ws_ref, bs_ref, o_ref):
    uv = jax.nn.gelu(p_ref[...])
    w = uv.shape[1] // 2
    u = uv[:, :w]
    v = uv[:, w:]
    mu = jnp.mean(v, axis=-1, keepdims=True)
    var = jnp.mean((v - mu) ** 2, axis=-1, keepdims=True)
    v = ((v - mu) * lax.rsqrt(var + EPS) * g_ref[...] + b_ref[...]).astype(BF16)
    lane = lax.broadcasted_iota(I32, (CHUNK, w), 1)
    head = lane // (w // GMLP_HEADS)
    for c in range(TP // CHUNK):
        vc = v[c * CHUNK:(c + 1) * CHUNK]
        sv = bs_ref[...]
        for hh in range(GMLP_HEADS):
            sv = sv + jnp.where(head == hh, _dot(ws_ref[hh], vc), 0.0)
        o_ref[c * CHUNK:(c + 1) * CHUNK, :] = (u[c * CHUNK:(c + 1) * CHUNK] * sv).astype(BF16)


def gmlp_mix(p1, ln_g, ln_b, ws_bf, bs_full, n_rows):
    return pl.pallas_call(
        _gmlp_kernel,
        grid=(n_rows // TP,),
        in_specs=[
            pl.BlockSpec((TP, 512), lambda j: (j, 0)),
            pl.BlockSpec((1, 256), lambda j: (0, 0)),
            pl.BlockSpec((1, 256), lambda j: (0, 0)),
            pl.BlockSpec((GMLP_HEADS, CHUNK, CHUNK), lambda j: (0, 0, 0)),
            pl.BlockSpec((CHUNK, 256), lambda j: (0, 0)),
        ],
        out_specs=pl.BlockSpec((TP, 256), lambda j: (j, 0)),
        out_shape=jax.ShapeDtypeStruct((n_rows, 256), BF16),
        compiler_params=_cparams(("arbitrary",)),
        name="gmlp_mix",
    )(p1, ln_g, ln_b, ws_bf, bs_full)


def s5_tables(a_re, a_im, log_dt, b_re, b_im, c_re, c_im):
    hp = lax.Precision.HIGHEST
    T = SSM_T
    a_re = a_re.astype(F32)
    a_im = a_im.astype(F32)
    dt = jnp.exp(log_dt.astype(F32))[..., None]
    kk = jnp.arange(T + 1, dtype=F32)[:, None, None, None]
    mag = jnp.exp(kk * (a_re * dt)[None])
    pw_re = mag * jnp.cos(kk * (a_im * dt)[None])
    pw_im = mag * jnp.sin(kk * (a_im * dt)[None])

    def cmul(xr, xi, yr, yi):
        return xr * yr - xi * yi, xr * yi + xi * yr

    nr, ni = pw_re[1] - 1.0, pw_im[1]
    den = a_re * a_re + a_im * a_im
    zr, zi = (nr * a_re + ni * a_im) / den, (ni * a_re - nr * a_im) / den
    bb_re, bb_im = cmul(zr[..., None], zi[..., None], b_re.astype(F32), b_im.astype(F32))
    c_re = c_re.astype(F32)
    c_im = c_im.astype(F32)
    G, P = c_re.shape[1], c_re.shape[2]
    e_re, e_im = cmul(c_re[None], c_im[None], pw_re[:T, :, :, None, :], pw_im[:T, :, :, None, :])
    kern = (jnp.einsum('kdgpn,dgnq->dkgpq', e_re, bb_re, precision=hp)
            - jnp.einsum('kdgpn,dgnq->dkgpq', e_im, bb_im, precision=hp))
    s_idx = jnp.arange(T)
    lag = s_idx[None, :] - s_idx[:, None]
    sel_f = (lag[None] == s_idx[:, None, None]).astype(F32)
    sel_b = (-lag[None] == s_idx[:, None, None]).astype(F32)
    toe = (jnp.einsum('kst,kgpq->gqspt', sel_f, kern[0], precision=hp)
           + jnp.einsum('kst,kgpq->gqspt', sel_b, kern[1], precision=hp)).reshape(G, P * T, P * T)
    def pack_s(pr, pi, d):
        zr_, zi_ = cmul(pr[..., None], pi[..., None], bb_re[d][None], bb_im[d][None])
        f = lambda z: z.transpose(1, 3, 0, 2).reshape(G, P * T, -1)
        return f(zr_), f(zi_)

    sfr, sfi = pack_s(pw_re[T - 1 - s_idx, 0], pw_im[T - 1 - s_idx, 0], 0)
    sbr, sbi = pack_s(pw_re[s_idx, 1], pw_im[s_idx, 1], 1)
    sall = jnp.concatenate([sfr, sfi, sfi, sfr, sbr, sbi, sbi, sbr], axis=-1)
    def pack_w(pr, pi, d):
        zr_, zi_ = cmul(c_re[d][:, :, :, None], c_im[d][:, :, :, None],
                        pr.transpose(1, 2, 0)[:, None], pi.transpose(1, 2, 0)[:, None])
        f = lambda z: z.transpose(0, 2, 1, 3).reshape(G, -1, P * T)
        return jnp.concatenate([f(zr_), -f(zi_)], axis=1)

    wst = jnp.concatenate([pack_w(pw_re[1 + s_idx, 0], pw_im[1 + s_idx, 0], 0),
                           pack_w(pw_re[T - s_idx, 1], pw_im[T - s_idx, 1], 1)], axis=1)
    rows = []
    for dd in range(2):
        ar, ai = pw_re[T, dd], pw_im[T, dd]
        rows += [jnp.concatenate([ar, ar], -1), jnp.concatenate([-ai, ai], -1), jnp.concatenate([ai, -ai], -1)]
    rows += [jnp.zeros_like(rows[0])] * 2
    dec = jnp.stack(rows, axis=1)
    return toe.astype(BF16), sall.astype(BF16), wst.astype(BF16), dec.astype(F32)


def _s5_kernel(n_ctx_chunks, n_lat_chunks, nb, a_ref, toe_ref, sall_ref, wst_ref, dec_ref,
               o_ref, x_ref, h_ref):
    a = a_ref[0].astype(BF16)
    y = _dot(a, toe_ref[0])
    x = _dot(a, sall_ref[0])
    for i in range(4):
        x_ref[i] = x[:, i * LANES:(i + 1) * LANES]
    dec = dec_ref[0]
    n_chunks = n_ctx_chunks + n_lat_chunks
    fwd_order = list(range(n_lat_chunks, n_chunks)) + list(range(n_lat_chunks))
    bwd_order = list(range(n_chunks - 1, n_lat_chunks - 1, -1)) + list(range(n_lat_chunks - 1, -1, -1))
    for d, order in enumerate((fwd_order, bwd_order)):
        a1 = dec[3 * d:3 * d + 1]
        a2 = dec[3 * d + 1:3 * d + 2]
        a3 = dec[3 * d + 2:3 * d + 3]
        s1 = jnp.zeros((nb, LANES), F32)
        s2 = jnp.zeros((nb, LANES), F32)
        for c in order:
            if c < n_lat_chunks:
                r = pl.ds(c, nb, stride=n_lat_chunks)
            else:
                r = pl.ds(nb * n_lat_chunks + c - n_lat_chunks, nb, stride=n_ctx_chunks)
            h_ref[d, r, :] = s1
            x1 = x_ref[2 * d, r, :]
            x2 = x_ref[2 * d + 1, r, :]
            s1, s2 = a1 * s1 + a2 * s2 + x1, a1 * s2 + a3 * s1 + x2
    for d in range(2):
        y = y + _dot(h_ref[d].astype(BF16), wst_ref[0, d * LANES:(d + 1) * LANES, :])
    o_ref[0] = y


def s5_core(a_g, toe, sall, wst, dec, n_ctx_chunks, n_lat_chunks, nb):
    G, rows, w = a_g.shape
    return pl.pallas_call(
        functools.partial(_s5_kernel, n_ctx_chunks, n_lat_chunks, nb),
        grid=(G,),
        in_specs=[
            pl.BlockSpec((1, rows, w), lambda g: (g, 0, 0)),
            pl.BlockSpec((1, w, w), lambda g: (g, 0, 0)),
            pl.BlockSpec((1, w, 512), lambda g: (g, 0, 0)),
            pl.BlockSpec((1, 256, w), lambda g: (g, 0, 0)),
            pl.BlockSpec((1, 8, LANES), lambda g: (g, 0, 0)),
        ],
        out_specs=pl.BlockSpec((1, rows, w), lambda g: (g, 0, 0)),
        out_shape=jax.ShapeDtypeStruct((G, rows, w), F32),
        scratch_shapes=[pltpu.VMEM((4, rows, LANES), F32), pltpu.VMEM((2, rows, LANES), F32)],
        compiler_params=_cparams(("arbitrary",)),
        name="s5_core",
    )(a_g, toe, sall, wst, dec)


def s5_to_groups(u):
    T = SSM_T
    G = u.shape[1] // SSM_CH
    return u.reshape(-1, T, G, SSM_CH).transpose(2, 0, 3, 1).reshape(G, -1, SSM_CH * T)


def s5_from_groups(y):
    T = SSM_T
    G = y.shape[0]
    return y.reshape(G, -1, SSM_CH, T).transpose(1, 3, 0, 2).reshape(-1, G * SSM_CH)


def _s5_post_kernel(y_ref, u_ref, d_ref, w_ref, o_ref):
    y = jax.nn.gelu(y_ref[...] + d_ref[...] * u_ref[...])
    z = _dot(y.astype(BF16), w_ref[...])
    w = z.shape[1] // 2
    o_ref[...] = (z[:, :w] * jax.nn.sigmoid(z[:, w:])).astype(BF16)


def s5_post(y_nat, pu, d, w_glu_bf, n_rows):
    nt = n_rows
    return pl.pallas_call(
        _s5_post_kernel,
        grid=(nt // TM,),
        in_specs=[
            pl.BlockSpec((TM, 256), lambda j: (j, 0)),
            pl.BlockSpec((TM, 256), lambda j: (j, 0)),
            pl.BlockSpec((1, 256), lambda j: (0, 0)),
            pl.BlockSpec((256, 512), lambda j: (0, 0)),
        ],
        out_specs=pl.BlockSpec((TM, 256), lambda j: (j, 0)),
        out_shape=jax.ShapeDtypeStruct((nt, 256), BF16),
        compiler_params=_cparams(("arbitrary",)),
        name="s5_post",
    )(y_nat, pu, d, w_glu_bf)


def na_bias_table(rpb):
    ci = jnp.arange(GRID_W)
    cs = jnp.clip(ci - WIN_COLS // 2, 0, GRID_W - WIN_COLS)
    kc = jnp.arange(GRID_W)
    inside = (kc[None, :] >= cs[:, None]) & (kc[None, :] < cs[:, None] + WIN_COLS)
    col_off = kc[None, :] - ci[:, None] + (WIN_COLS - 1)
    sel = (col_off[:, :, None] == jnp.arange(2 * WIN_COLS - 1)[None, None, :]).astype(F32)
    base = jnp.einsum('hrw,qkw->hrqk', rpb.astype(F32), sel, precision=lax.Precision.HIGHEST)
    base = jnp.where(inside[None, None], base, NEG)
    tab = jnp.stack([base[:, WIN_ROWS - 1 - ty:2 * WIN_ROWS - 1 - ty] for ty in range(WIN_ROWS)], axis=0)
    tab = tab.transpose(0, 1, 3, 2, 4)
    return tab.reshape(WIN_ROWS, NA_HEADS * GRID_W, WIN_ROWS * GRID_W)


def _heads_stack(q):
    lane = lax.broadcasted_iota(I32, q.shape, 1)
    zero = jnp.zeros_like(q)
    return jnp.concatenate([jnp.where(lane // HEAD_DIM == hh, q, zero) for hh in range(NA_HEADS)], axis=0)


def _heads_merge(o4, m):
    lane = lax.broadcasted_iota(I32, (m, o4.shape[1]), 1)
    out = jnp.zeros((m, o4.shape[1]), F32)
    for hh in range(NA_HEADS):
        out = jnp.where(lane // HEAD_DIM == hh, o4[hh * m:(hh + 1) * m], out)
    return out


def _na_kernel(rows_per_step, n_rows, lat_steps, q_ref, k_ref, v_ref, kc_ref, vc_ref, tab_ref, o_ref):
    i = pl.program_id(1)
    kc = kc_ref[...]
    vc = vc_ref[...]
    nkeys = WIN_ROWS * GRID_W

    @pl.when(i < lat_steps)
    def _():
        def body(a, carry):
            r = i * rows_per_step + a
            rs = jnp.clip(r - WIN_ROWS // 2, 0, n_rows - WIN_ROWS)
            ty = r - rs
            q0 = pl.multiple_of(a * GRID_W, GRID_W)
            k0 = pl.multiple_of(rs * GRID_W, GRID_W)
            q4 = _heads_stack(q_ref[pl.ds(q0, GRID_W), :])
            kw = k_ref[pl.ds(k0, nkeys), :]
            vw = v_ref[pl.ds(k0, nkeys), :]
            s = _dot_nt(q4, kw) + tab_ref[ty]
            sc = _dot_nt(q4, kc)
            m = jnp.maximum(jnp.max(s, axis=-1, keepdims=True), jnp.max(sc, axis=-1, keepdims=True))
            e = jnp.exp(s - m)
            ec = jnp.exp(sc - m)
            den = jnp.sum(e, axis=-1, keepdims=True) + jnp.sum(ec, axis=-1, keepdims=True)
            o4 = (_dot(e.astype(BF16), vw) + _dot(ec.astype(BF16), vc)) / den
            o_ref[pl.ds(q0, GRID_W), :] = _heads_merge(o4, GRID_W).astype(BF16)
            return carry

        lax.fori_loop(0, rows_per_step, body, 0)

    @pl.when(i >= lat_steps)
    def _():
        q4 = _heads_stack(q_ref[...])
        s = _dot_nt(q4, kc)
        m = jnp.max(s, axis=-1, keepdims=True)
        e = jnp.exp(s - m)
        den = jnp.sum(e, axis=-1, keepdims=True)
        o4 = _dot(e.astype(BF16), vc) / den
        o_ref[...] = _heads_merge(o4, q_ref.shape[0]).astype(BF16)


def na_attention(q, k, v, tab, nb, seq_len, ctx_len, with_ctx):
    tq = ctx_len
    rows_per_step = tq // GRID_W
    n_rows = seq_len // GRID_W
    lat_steps = seq_len // tq
    ctx_blk0 = nb * seq_len // ctx_len

    def q_blk(b, i):
        return (jnp.where(i < lat_steps, b * lat_steps + i, ctx_blk0 + b), 0)

    n_out = nb * seq_len + (nb * ctx_len if with_ctx else 0)
    return pl.pallas_call(
        functools.partial(_na_kernel, rows_per_step, n_rows, lat_steps),
        grid=(nb, lat_steps + (1 if with_ctx else 0)),
        in_specs=[
            pl.BlockSpec((tq, 256), q_blk),
            pl.BlockSpec((seq_len, 256), lambda b, i: (b, 0)),
            pl.BlockSpec((seq_len, 256), lambda b, i: (b, 0)),
            pl.BlockSpec((ctx_len, 256), lambda b, i: (ctx_blk0 + b, 0)),
            pl.BlockSpec((ctx_len, 256), lambda b, i: (ctx_blk0 + b, 0)),
            pl.BlockSpec(tab.shape, lambda b, i: (0, 0, 0)),
        ],
        out_specs=pl.BlockSpec((tq, 256), q_blk),
        out_shape=jax.ShapeDtypeStruct((n_out, 256), BF16),
        compiler_params=_cparams(("arbitrary", "arbitrary")),
        name="na_attention",
    )(q, k, v, k, v, tab)


def _merge_kernel(tiles_per_seq, n_lat_tiles, x_ref, mod_ref, gpre_ref, gpost_ref, gffn_ref,
                  ya_ref, yb_ref, yg_ref, yd_ref, wbg_ref, bbg_ref, wbr_ref, wout_ref,
                  wrh_ref, wrl_ref, br_ref, tri_ref, xn_ref, hf_ref, ti_ref, rk_ref, tw_ref, cnt_ref,
                  run_ref):
    x = x_ref[...]
    d = x.shape[1]
    mod = mod_ref[0]
    h = _rms(x, gpre_ref[...]) * (1.0 + mod[1:2]) + mod[0:1]
    hb = h.astype(BF16)
    acc = jnp.zeros_like(x)
    for bi, br_ref_i in enumerate((ya_ref, yb_ref, yg_ref, yd_ref)):
        gate = jax.nn.sigmoid(_dot(hb, wbg_ref[:, bi * d:(bi + 1) * d]) + bbg_ref[:, bi * d:(bi + 1) * d])
        acc = acc + gate * _dot(br_ref_i[...], wbr_ref[bi])
    y = _dot(acc.astype(BF16), wout_ref[...])
    xn = x + mod[2:3] * _rms(y, gpost_ref[...])
    xn_ref[...] = xn
    hf = _rms(xn, gffn_ref[...]) * (1.0 + mod[4:5]) + mod[3:4]
    hf_hi = hf.astype(BF16)
    bits = lax.bitcast_convert_type(hf_hi.astype(F32), jnp.uint32)
    hf_ref[...] = (bits[:, :d // 2] >> 16) | (bits[:, d // 2:] & jnp.uint32(0xFFFF0000))
    hf_lo = (hf - hf_hi.astype(F32)).astype(BF16)
    logits = _dot(hf_hi, wrh_ref[...]) + (_dot(hf_hi, wrl_ref[...]) + _dot(hf_lo, wrh_ref[...])) + br_ref[...]

    @pl.when(pl.program_id(0) == 0)
    def _():
        run_ref[...] = jnp.zeros_like(run_ref)

    ne = run_ref.shape[0]
    tm = logits.shape[0]
    work = jnp.transpose(logits)[:ne]
    erow = lax.broadcasted_iota(I32, work.shape, 0)
    idxs, es, ohs = [], [], []
    v0 = None
    for kk in range(TOP_K):
        m = jnp.max(work, axis=0, keepdims=True)
        idx = jnp.min(jnp.where(work == m, erow, ne), axis=0, keepdims=True)
        oh = erow == idx
        if kk == 0:
            v0 = m
            es.append(jnp.ones_like(m))
        else:
            es.append(jnp.exp(m - v0))
        idxs.append(idx)
        ohs.append(oh)
        work = jnp.where(oh, -jnp.inf, work)
    den = es[0] + es[1] + es[2] + es[3]
    ohf = [o.astype(F32) for o in ohs]
    cnt = _dot(jnp.concatenate([o.astype(BF16) for o in ohf], axis=0), tri_ref[...])
    base = run_ref[:, 0:1]
    r8 = lax.broadcasted_iota(I32, (8, tm), 0)
    r128 = lax.broadcasted_iota(I32, (LANES, tm), 0)
    ti8 = jnp.zeros((8, tm), I32)
    rk8 = jnp.zeros((8, tm), I32)
    twt = jnp.zeros((LANES, tm), F32)
    for kk in range(TOP_K):
        rank = jnp.sum(ohf[kk] * (base + cnt[kk * ne:(kk + 1) * ne]), axis=0, keepdims=True)
        base = base + jnp.sum(ohf[kk], axis=1, keepdims=True)
        ti8 = jnp.where(r8 == kk, idxs[kk], ti8)
        rk8 = jnp.where(r8 == kk, rank.astype(I32), rk8)
        twt = jnp.where(r128 == kk, es[kk] / den, twt)
    run_ref[...] = jnp.broadcast_to(base, run_ref.shape)
    ti_ref[...] = ti8
    rk_ref[...] = rk8
    tw_ref[...] = jnp.transpose(twt)
    cnt_ref[...] = run_ref[...]


def merge_and_route(x, mod, gpre, gpost, gffn, ya, yb, yg, yd, wbg, bbg, wbr, wout, wrh, wrl, br,
                    n_tokens, n_lat, seq_len, n_experts):
    d = x.shape[1]
    n_lat_tiles = n_lat // TM
    tiles_per_seq = seq_len // TM

    def mod_row(j):
        return jnp.where(j < n_lat_tiles, j // tiles_per_seq, 4)

    tri = jnp.triu(jnp.ones((TM, TM), F32), 1).astype(BF16)
    tok = lambda w: pl.BlockSpec((TM, w), lambda j: (j, 0))
    lanes = pl.BlockSpec((8, TM), lambda j: (0, j))
    full = lambda shape: pl.BlockSpec(shape, lambda j: tuple(0 for _ in shape))
    return pl.pallas_call(
        functools.partial(_merge_kernel, tiles_per_seq, n_lat_tiles),
        grid=(n_tokens // TM,),
        in_specs=[
            tok(d),
            pl.BlockSpec((1, 6, d), lambda j: (mod_row(j), 0, 0)),
            full((1, d)), full((1, d)), full((1, d)),
            tok(256), tok(256), tok(256), tok(256),
            full(wbg.shape), full(bbg.shape), full(wbr.shape), full(wout.shape),
            full(wrh.shape), full(wrl.shape), full(br.shape), full(tri.shape),
        ],
        out_specs=[tok(d), tok(d // 2), lanes, lanes, tok(LANES), full((n_experts, LANES))],
        out_shape=[
            jax.ShapeDtypeStruct((n_tokens, d), F32),
            jax.ShapeDtypeStruct((n_tokens, d // 2), jnp.uint32),
            jax.ShapeDtypeStruct((8, n_tokens), I32),
            jax.ShapeDtypeStruct((8, n_tokens), I32),
            jax.ShapeDtypeStruct((n_tokens, LANES), F32),
            jax.ShapeDtypeStruct((n_experts, LANES), F32),
        ],
        scratch_shapes=[pltpu.VMEM((n_experts, LANES), F32)],
        compiler_params=_cparams(("arbitrary",)),
        name="merge_and_route",
    )(x, mod, gpre, gpost, gffn, ya, yb, yg, yd, wbg, bbg, wbr, wout, wrh, wrl, br, tri)


def _expert_kernel(be_ref, nv_ref, x_ref, w1_ref, b1_ref, w2_ref, b2_ref, o_ref, w1b_ref, w2b_ref):
    i = pl.program_id(0)
    prev = be_ref[jnp.maximum(i - 1, 0)]
    changed = jnp.logical_or(i == 0, be_ref[i] != prev)

    @pl.when(jnp.logical_and(i < nv_ref[0], changed))
    def _():
        w1b_ref[...] = w1_ref[0].astype(BF16)
        w2b_ref[...] = w2_ref[0].astype(BF16)

    @pl.when(i < nv_ref[0])
    def _():
        w = x_ref[...]
        half = w.shape[1]
        x_lo = lax.bitcast_convert_type(w << 16, F32).astype(BF16)
        x_hi = lax.bitcast_convert_type(w & jnp.uint32(0xFFFF0000), F32).astype(BF16)
        h = _dot(x_lo, w1b_ref[0:half, :]) + _dot(x_hi, w1b_ref[half:, :]) + b1_ref[0]
        de = h.shape[1] // 2
        g = jnp.minimum(h[:, :de], SWIGLU_LIMIT)
        lin = jnp.clip(h[:, de:], -SWIGLU_LIMIT, SWIGLU_LIMIT)
        act = g * jax.nn.sigmoid(SWIGLU_ALPHA * g) * (lin + 1.0)
        o_ref[...] = _dot(act.astype(BF16), w2b_ref[...]) + b2_ref[0]

    @pl.when(i >= nv_ref[0])
    def _():
        o_ref[...] = jnp.zeros_like(o_ref)


def moe_experts(xb, block_e, n_valid, w1, b1, w2, b2, layer):
    p = xb.shape[0]
    depth, ne, d, de2 = w1.shape
    n_blocks = p // MOE_BM
    grid_spec = pltpu.PrefetchScalarGridSpec(
        num_scalar_prefetch=2,
        grid=(n_blocks,),
        in_specs=[
            pl.BlockSpec((MOE_BM, d // 2), lambda i, be, nv: (i, 0)),
            pl.BlockSpec((None, 1, d, de2), lambda i, be, nv: (layer, be[i], 0, 0)),
            pl.BlockSpec((None, 1, 1, de2), lambda i, be, nv: (layer, be[i], 0, 0)),
            pl.BlockSpec((None, 1, de2 // 2, d), lambda i, be, nv: (layer, be[i], 0, 0)),
            pl.BlockSpec((None, 1, 1, d), lambda i, be, nv: (layer, be[i], 0, 0)),
        ],
        out_specs=pl.BlockSpec((MOE_BM, d), lambda i, be, nv: (i, 0)),
        scratch_shapes=[pltpu.VMEM((d, de2), BF16), pltpu.VMEM((de2 // 2, d), BF16)],
    )
    return pl.pallas_call(
        _expert_kernel,
        grid_spec=grid_spec,
        out_shape=jax.ShapeDtypeStruct((p, d), F32),
        compiler_params=_cparams(("arbitrary",)),
        name="moe_experts",
    )(block_e, n_valid, xb, w1, b1.reshape(depth, ne, 1, de2), w2, b2.reshape(depth, ne, 1, d))


def _combine_kernel(dcur_ref, dnext_ref, x_ref, mod_ref, g_ref, tw_ref, yblk_ref, o_ref, gbuf, sem):
    j = pl.program_id(0)
    n = pl.num_programs(0)
    slot = j % 2
    tm = x_ref.shape[0]

    def row_copy(dref, s, t, kk):
        return pltpu.make_async_copy(yblk_ref.at[pl.ds(dref[kk, t], 1)], gbuf.at[s, kk, pl.ds(t, 1)], sem.at[s])

    def issue(dref, s):
        def body(t, carry):
            for kk in range(TOP_K):
                row_copy(dref, s, t, kk).start()
            return carry
        lax.fori_loop(0, tm, body, 0, unroll=DMA_UNROLL)

    @pl.when(j == 0)
    def _():
        issue(dcur_ref, slot)

    @pl.when(j + 1 < n)
    def _():
        issue(dnext_ref, 1 - slot)

    def wait(t, carry):
        for kk in range(TOP_K):
            row_copy(dcur_ref, slot, t, kk).wait()
        return carry
    lax.fori_loop(0, tm, wait, 0, unroll=DMA_UNROLL)

    tw = tw_ref[...]
    y = tw[:, 0:1] * gbuf[slot, 0]
    for kk in range(1, TOP_K):
        y = y + tw[:, kk:kk + 1] * gbuf[slot, kk]
    o_ref[...] = x_ref[...] + mod_ref[0, 5:6, :] * _rms(y, g_ref[...])


def moe_combine(dest_t, xn, mod, gain, tw, yblk, n_tokens, n_lat, seq_len):
    d = xn.shape[1]
    n_lat_tiles = n_lat // TM
    tiles_per_seq = seq_len // TM
    n_tiles = n_tokens // TM

    def mod_row(j):
        return jnp.where(j < n_lat_tiles, j // tiles_per_seq, 4)

    return pl.pallas_call(
        _combine_kernel,
        grid=(n_tiles,),
        in_specs=[
            pl.BlockSpec((8, TM), lambda j: (0, j), memory_space=pltpu.SMEM),
            pl.BlockSpec((8, TM), lambda j: (0, jnp.minimum(j + 1, n_tiles - 1)), memory_space=pltpu.SMEM),
            pl.BlockSpec((TM, d), lambda j: (j, 0)),
            pl.BlockSpec((1, 6, d), lambda j: (mod_row(j), 0, 0)),
            pl.BlockSpec((1, d), lambda j: (0, 0)),
            pl.BlockSpec((TM, LANES), lambda j: (j, 0)),
            pl.BlockSpec(memory_space=pl.ANY),
        ],
        out_specs=pl.BlockSpec((TM, d), lambda j: (j, 0)),
        out_shape=jax.ShapeDtypeStruct((n_tokens, d), F32),
        scratch_shapes=[pltpu.VMEM((2, TOP_K, TM, d), F32), pltpu.SemaphoreType.DMA((2,))],
        compiler_params=_cparams(("arbitrary",)),
        name="moe_combine",
    )(dest_t, dest_t, xn, mod, gain, tw, yblk)


def _dest_kernel(n_experts, ps_ref, ti_ref, rk_ref, o_ref):
    ti = ti_ref[...]
    dest = rk_ref[...]
    for e in range(n_experts):
        dest = dest + jnp.where(ti == e, ps_ref[e], 0)
    o_ref[...] = dest


def moe_dest(pstart, ti_t, rk_t):
    n_tokens = ti_t.shape[1]
    lanes = pl.BlockSpec((8, TM), lambda j, ps: (0, j))
    return pl.pallas_call(
        functools.partial(_dest_kernel, pstart.shape[0]),
        grid_spec=pltpu.PrefetchScalarGridSpec(
            num_scalar_prefetch=1, grid=(n_tokens // TM,), in_specs=[lanes, lanes], out_specs=lanes),
        out_shape=jax.ShapeDtypeStruct(ti_t.shape, I32),
        compiler_params=_cparams(("arbitrary",)),
        name="moe_dest",
    )(pstart, ti_t, rk_t)


def _dispatch_kernel(dest_ref, hf_ref, xb_in_ref, xb_ref, sem):
    del xb_in_ref
    tm = hf_ref.shape[0]

    def row_copy(t, kk):
        return pltpu.make_async_copy(hf_ref.at[pl.ds(t, 1)], xb_ref.at[pl.ds(dest_ref[kk, t], 1)], sem)

    def start(t, carry):
        for kk in range(TOP_K):
            row_copy(t, kk).start()
        return carry
    lax.fori_loop(0, tm, start, 0, unroll=DMA_UNROLL)

    def wait(t, carry):
        for kk in range(TOP_K):
            row_copy(t, kk).wait()
        return carry
    lax.fori_loop(0, tm, wait, 0, unroll=DMA_UNROLL)


def moe_dispatch(dest_t, hfp, n_rows):
    n_tokens, w = hfp.shape
    xb0 = jnp.zeros((n_rows, w), hfp.dtype)
    return pl.pallas_call(
        _dispatch_kernel,
        grid=(n_tokens // TM,),
        in_specs=[
            pl.BlockSpec((8, TM), lambda j: (0, j), memory_space=pltpu.SMEM),
            pl.BlockSpec((TM, w), lambda j: (j, 0)),
            pl.BlockSpec(memory_space=pl.ANY),
        ],
        out_specs=pl.BlockSpec(memory_space=pl.ANY),
        out_shape=jax.ShapeDtypeStruct((n_rows, w), hfp.dtype),
        input_output_aliases={2: 0},
        scratch_shapes=[pltpu.SemaphoreType.DMA(())],
        compiler_params=_cparams(("arbitrary",)),
        name="moe_dispatch",
    )(dest_t, hfp, xb0)


def kernel(x, c, ctx, c_ctx, ada_w, ada_b, norm_mix_pre, norm_mix_post, norm_ffn_pre, norm_ffn_post,
           w_in, w_branch_gate, b_branch_gate, w_branch, w_out, pool_w, pool_scale,
           ssm_a_re, ssm_a_im, ssm_log_dt, ssm_b_re, ssm_b_im, ssm_c_re, ssm_c_im, ssm_d, ssm_w_glu,
           gmlp_ln_g, gmlp_ln_b, gmlp_w_s, gmlp_b_s, na_rpb,
           router_w, router_b, expert_w1, expert_b1, expert_w2, expert_b2):
    nb, seq_len, d = x.shape
    ctx_len = ctx.shape[1]
    depth = ada_w.shape[0]
    n_lat = nb * seq_len
    n_all = n_lat + nb * ctx_len
    n_experts = router_w.shape[-1]
    assert nb == 4 and seq_len % TM == 0 and (nb * ctx_len) % TM == 0 and ctx_len % TP == 0

    xs = jnp.concatenate([x.reshape(n_lat, d), ctx.reshape(nb * ctx_len, d)], axis=0)
    cvec = jnp.concatenate([c, c_ctx[None, :], jnp.zeros((3, d), F32)], axis=0)
    mods = ada_modulation(cvec, ada_w, ada_b).reshape(depth, 8, 6, d)
    cos_t, sin_t = rope_tables(seq_len)

    out = None
    for l in range(depth):
        last = l == depth - 1
        mod = mods[l]
        row = lambda v: v.reshape(1, -1)
        pa, pu, pg, q, k, v = premix(xs, mod, row(norm_mix_pre[l]), w_in[l].astype(BF16), cos_t, sin_t,
                                     n_lat, seq_len)

        n_mix = n_lat if last else n_all
        w_bd = jax.scipy.linalg.block_diag(*[pool_w[l, g] for g in range(POOL_GROUPS)]).astype(BF16)
        ya = pool_mix(pa, w_bd, row(pool_scale[l]), n_mix, n_lat, seq_len, ctx_len)

        bs_full = jnp.repeat(gmlp_b_s[l].T, 256 // GMLP_HEADS, axis=1)
        yg = gmlp_mix(pg, row(gmlp_ln_g[l]), row(gmlp_ln_b[l]), gmlp_w_s[l].astype(BF16), bs_full, n_mix)

        toe, sall, wst, dec = s5_tables(ssm_a_re[l], ssm_a_im[l], ssm_log_dt[l], ssm_b_re[l], ssm_b_im[l],
                                        ssm_c_re[l], ssm_c_im[l])
        y_g = s5_core(s5_to_groups(pu), toe, sall, wst, dec, ctx_len // SSM_T, seq_len // SSM_T, nb)
        yb = s5_post(s5_from_groups(y_g), pu, row(ssm_d[l]), ssm_w_glu[l].astype(BF16), n_mix)

        yd = na_attention(q, k, v, na_bias_table(na_rpb[l]), nb, seq_len, ctx_len, not last)

        rw = jnp.pad(router_w[l], ((0, 0), (0, LANES - n_experts)))
        rw_hi = rw.astype(BF16)
        rw_lo = (rw - rw_hi.astype(F32)).astype(BF16)
        rb = jnp.pad(router_b[l], (0, LANES - n_experts), constant_values=NEG).reshape(1, LANES)
        n_tok = n_lat if last else n_all
        xn, hfp, ti_t, rk_t, tw, counts = merge_and_route(
            xs, mod, row(norm_mix_pre[l]), row(norm_mix_post[l]), row(norm_ffn_pre[l]),
            ya, yb, yg, yd, w_branch_gate[l].astype(BF16), row(b_branch_gate[l]),
            w_branch[l].astype(BF16), w_out[l].astype(BF16), rw_hi, rw_lo, rb, n_tok, n_lat, seq_len, n_experts)

        cnt = counts[:, 0].astype(I32)
        padded = (cnt + MOE_BM - 1) // MOE_BM * MOE_BM
        pend = jnp.cumsum(padded)
        n_blocks = n_tok * TOP_K // MOE_BM + n_experts
        blk_start = jnp.arange(n_blocks, dtype=I32) * MOE_BM
        block_e = jnp.minimum(jnp.sum((pend[None, :] <= blk_start[:, None]).astype(I32), axis=1), n_experts - 1)
        n_valid = (pend[-1] // MOE_BM).astype(I32).reshape(1)

        dest_t = moe_dest(pend - padded, ti_t, rk_t)
        xb = moe_dispatch(dest_t, hfp, n_blocks * MOE_BM)
        yblk = moe_experts(xb, block_e, n_valid, expert_w1, expert_b1, expert_w2, expert_b2, l)
        xs = moe_combine(dest_t, xn, mod, row(norm_ffn_post[l]), tw, yblk, n_tok, n_lat, seq_len)
        out = xs
    return out.reshape(nb, seq_len, d)
```

```python
import functools
import math

import jax
import jax.numpy as jnp
import jax.scipy.linalg
from jax import lax
from jax.experimental import pallas as pl
from jax.experimental.pallas import tpu as pltpu

F32 = jnp.float32
BF16 = jnp.bfloat16
I32 = jnp.int32

GRID_W = 64
EPS = 1e-6
POOL_GROUPS = 4
POOL_WINDOWS = (2, 4, 8, 16)
SSM_CH = 16
SSM_N = 64
GMLP_HEADS = 4
CHUNK = 128
NA_HEADS = 4
HEAD_DIM = 64
WIN_ROWS = 8
WIN_COLS = 16
ROPE_BASE = 10000.0
BRANCH_W = 256
N_BRANCH = 4
TOP_K = 4
SWIGLU_LIMIT = 7.0
SWIGLU_ALPHA = 1.702

TM = 512
TP = 256
POOL_HALO = 16
SSM_T = 32
MOE_BM = 256
DMA_UNROLL = 8
LANES = 128
NEG = -1e30
VMEM_LIMIT = 56 * 1024 * 1024


def _cparams(sem):
    return pltpu.CompilerParams(dimension_semantics=sem, vmem_limit_bytes=VMEM_LIMIT)


def _dot(a, b):
    return jnp.dot(a, b, preferred_element_type=F32)


def _dot_nt(a, b):
    return lax.dot_general(a, b, (((1,), (1,)), ((), ())), preferred_element_type=F32)


def _ada_kernel(c_ref, w_ref, b_ref, o_ref):
    c = c_ref[...]
    s = c * jax.nn.sigmoid(c)
    o_ref[0] = jnp.dot(s, w_ref[0], preferred_element_type=F32,
                       precision=lax.Precision.HIGHEST) + b_ref[0]


def ada_modulation(cvec, ada_w, ada_b):
    depth, d, n = ada_w.shape
    tn = 1536
    return pl.pallas_call(
        _ada_kernel,
        grid=(depth, n // tn),
        in_specs=[
            pl.BlockSpec((8, d), lambda l, j: (0, 0)),
            pl.BlockSpec((1, d, tn), lambda l, j: (l, 0, j)),
            pl.BlockSpec((1, 1, tn), lambda l, j: (l, 0, j)),
        ],
        out_specs=pl.BlockSpec((1, 8, tn), lambda l, j: (l, 0, j)),
        out_shape=jax.ShapeDtypeStruct((depth, 8, n), F32),
        compiler_params=_cparams(("arbitrary", "arbitrary")),
        name="ada_modulation",
    )(cvec, ada_w, ada_b.reshape(depth, 1, n))


def _rms(x, g):
    return x * lax.rsqrt(jnp.mean(x * x, axis=-1, keepdims=True) + EPS) * g


def _premix_kernel(n_lat_tiles, x_ref, mod_ref, g_ref, w_ref, cos_ref, sin_ref,
                   pa_ref, pu_ref, pg_ref, q_ref, k_ref, v_ref):
    j = pl.program_id(0)
    x = x_ref[...]
    h = _rms(x, g_ref[...])
    h = h * (1.0 + mod_ref[0, 1:2, :]) + mod_ref[0, 0:1, :]
    p = _dot(h.astype(BF16), w_ref[...])
    wa = p.shape[1] - 768
    pa_ref[...] = p[:, 0:256]
    pu_ref[...] = p[:, 256:512]
    pg_ref[...] = p[:, 512:wa]
    q = p[:, wa:wa + 256]
    k = p[:, wa + 256:wa + 512]
    v = p[:, wa + 512:wa + 768]
    is_lat = j < n_lat_tiles
    cos = jnp.where(is_lat, cos_ref[...], 1.0)
    sin = jnp.where(is_lat, sin_ref[...], 0.0)
    lane = lax.broadcasted_iota(I32, q.shape, 1)
    first = (lane % 32) < 16

    def rope(t):
        partner = jnp.where(first, pltpu.roll(t, 256 - 16, axis=1), pltpu.roll(t, 16, axis=1))
        return t * cos + partner * sin

    q_ref[...] = (rope(q) * (HEAD_DIM ** -0.5)).astype(BF16)
    k_ref[...] = rope(k).astype(BF16)
    v_ref[...] = v.astype(BF16)


def premix(x, mod, gain, w_in_bf, cos_t, sin_t, n_lat, seq_len):
    nt, d = x.shape
    in_w = w_in_bf.shape[1]
    wa = in_w - 768
    n_lat_tiles = n_lat // TM
    tiles_per_seq = seq_len // TM

    def mod_row(j):
        return jnp.where(j < n_lat_tiles, j // tiles_per_seq, 4)

    def tab_blk(j):
        return jnp.where(j < n_lat_tiles, j % tiles_per_seq, 0)

    return pl.pallas_call(
        functools.partial(_premix_kernel, n_lat_tiles),
        grid=(nt // TM,),
        in_specs=[
            pl.BlockSpec((TM, d), lambda j: (j, 0)),
            pl.BlockSpec((1, 6, d), lambda j: (mod_row(j), 0, 0)),
            pl.BlockSpec((1, d), lambda j: (0, 0)),
            pl.BlockSpec((d, in_w), lambda j: (0, 0)),
            pl.BlockSpec((TM, 256), lambda j: (tab_blk(j), 0)),
            pl.BlockSpec((TM, 256), lambda j: (tab_blk(j), 0)),
        ],
        out_specs=[
            pl.BlockSpec((TM, 256), lambda j: (j, 0)),
            pl.BlockSpec((TM, 256), lambda j: (j, 0)),
            pl.BlockSpec((TM, wa - 512), lambda j: (j, 0)),
            pl.BlockSpec((TM, 256), lambda j: (j, 0)),
            pl.BlockSpec((TM, 256), lambda j: (j, 0)),
            pl.BlockSpec((TM, 256), lambda j: (j, 0)),
        ],
        out_shape=[
            jax.ShapeDtypeStruct((nt, 256), F32),
            jax.ShapeDtypeStruct((nt, 256), F32),
            jax.ShapeDtypeStruct((nt, wa - 512), F32),
            jax.ShapeDtypeStruct((nt, 256), BF16),
            jax.ShapeDtypeStruct((nt, 256), BF16),
            jax.ShapeDtypeStruct((nt, 256), BF16),
        ],
        compiler_params=_cparams(("arbitrary",)),
        name="premix",
    )(x, mod, gain, w_in_bf, cos_t, sin_t)


def rope_tables(seq_len):
    t = jnp.arange(seq_len)
    half = HEAD_DIM // 2
    nf = half // 2
    inv = ROPE_BASE ** (-jnp.arange(nf, dtype=F32) / nf)
    d = jnp.arange(HEAD_DIM)
    pos = jnp.where((d // half)[None, :] == 0, (t // GRID_W)[:, None], (t % GRID_W)[:, None]).astype(F32)
    ang = pos * inv[(d % half) % nf][None, :]
    sign = jnp.where((d % half) < nf, -1.0, 1.0)[None, :]
    cos = jnp.tile(jnp.cos(ang), (1, NA_HEADS))
    sin = jnp.tile(jnp.sin(ang) * sign, (1, NA_HEADS))
    return cos.astype(F32), sin.astype(F32)


def _pool_kernel(n_lat_tiles, tiles_per_seq, ctx_tiles_per_seq, cur_ref, prev_ref, next_ref,
                 w_ref, sc_ref, o_ref, ext_ref):
    j = pl.program_id(0)
    is_lat = j < n_lat_tiles
    t0 = jnp.where(is_lat, (j % tiles_per_seq) * TP, ((j - n_lat_tiles) % ctx_tiles_per_seq) * TP)
    slen = jnp.where(is_lat, tiles_per_seq * TP, ctx_tiles_per_seq * TP)
    u = cur_ref[...]
    h = POOL_HALO
    ext_ref[0:h, :] = jnp.where(t0 > 0, prev_ref[...], 0.0)
    ext_ref[h:h + TP, :] = u
    ext_ref[h + TP:h + TP + h, :] = jnp.where(t0 + TP < slen, next_ref[...], 0.0)
    t = t0 + lax.broadcasted_iota(I32, (TP, 1), 0)
    lane = lax.broadcasted_iota(I32, (TP, u.shape[1]), 1)
    grp = lane // (u.shape[1] // POOL_GROUPS)
    mean = jnp.zeros_like(u)
    for gi, w in enumerate(POOL_WINDOWS):
        acc = ext_ref[pl.ds(h - w // 2, TP), :]
        for o in range(1, w):
            acc = acc + ext_ref[pl.ds(h - w // 2 + o, TP), :]
        cnt = (jnp.minimum(t - w // 2 + w, slen) - jnp.maximum(t - w // 2, 0)).astype(F32)
        mean = jnp.where(grp == gi, acc / cnt, mean)
    y = _dot((mean - u).astype(BF16), w_ref[...]) * sc_ref[...]
    o_ref[...] = y.astype(BF16)


def pool_mix(p1, w_bd_bf, scale, n_rows, n_lat, seq_len, ctx_len):
    nt = n_rows
    hb = TP // POOL_HALO
    n_halo_blocks = p1.shape[0] // POOL_HALO
    return pl.pallas_call(
        functools.partial(_pool_kernel, n_lat // TP, seq_len // TP, ctx_len // TP),
        grid=(nt // TP,),
        in_specs=[
            pl.BlockSpec((TP, 256), lambda j: (j, 0)),
            pl.BlockSpec((POOL_HALO, 256), lambda j: (jnp.maximum(j * hb - 1, 0), 0)),
            pl.BlockSpec((POOL_HALO, 256), lambda j: (jnp.minimum((j + 1) * hb, n_halo_blocks - 1), 0)),
            pl.BlockSpec((256, 256), lambda j: (0, 0)),
            pl.BlockSpec((1, 256), lambda j: (0, 0)),
        ],
        out_specs=pl.BlockSpec((TP, 256), lambda j: (j, 0)),
        out_shape=jax.ShapeDtypeStruct((n_rows, 256), BF16),
        scratch_shapes=[pltpu.VMEM((TP + 2 * POOL_HALO, 256), F32)],
        compiler_params=_cparams(("arbitrary",)),
        name="pool_mix",
    )(p1, p1, p1, w_bd_bf, scale)


def _gmlp_kernel(p_ref, g_ref, b_ref, ws_ref, bs_ref, o_ref):
    uv = jax.nn.gelu(p_ref[...])
    w = uv.shape[1] // 2
    u = uv[:, :w]
    v = uv[:, w:]
    mu = jnp.mean(v, axis=-1, keepdims=True)
    var = jnp.mean((v - mu) ** 2, axis=-1, keepdims=True)
    v = ((v - mu) * lax.rsqrt(var + EPS) * g_ref[...] + b_ref[...]).astype(BF16)
    lane = lax.broadcasted_iota(I32, (CHUNK, w), 1)
    head = lane // (w // GMLP_HEADS)
    for c in range(TP // CHUNK):
        vc = v[c * CHUNK:(c + 1) * CHUNK]
        sv = bs_ref[...]
        for hh in range(GMLP_HEADS):
            sv = sv + jnp.where(head == hh, _dot(ws_ref[hh], vc), 0.0)
        o_ref[c * CHUNK:(c + 1) * CHUNK, :] = (u[c * CHUNK:(c + 1) * CHUNK] * sv).astype(BF16)


def gmlp_mix(p1, ln_g, ln_b, ws_bf, bs_full, n_rows):
    return pl.pallas_call(
        _gmlp_kernel,
        grid=(n_rows // TP,),
        in_specs=[
            pl.BlockSpec((TP, 512), lambda j: (j, 0)),
            pl.BlockSpec((1, 256), lambda j: (0, 0)),
            pl.BlockSpec((1, 256), lambda j: (0, 0)),
            pl.BlockSpec((GMLP_HEADS, CHUNK, CHUNK), lambda j: (0, 0, 0)),
            pl.BlockSpec((CHUNK, 256), lambda j: (0, 0)),
        ],
        out_specs=pl.BlockSpec((TP, 256), lambda j: (j, 0)),
        out_shape=jax.ShapeDtypeStruct((n_rows, 256), BF16),
        compiler_params=_cparams(("arbitrary",)),
        name="gmlp_mix",
    )(p1, ln_g, ln_b, ws_bf, bs_full)


def s5_tables(a_re, a_im, log_dt, b_re, b_im, c_re, c_im):
    hp = lax.Precision.HIGHEST
    T = SSM_T
    a_re = a_re.astype(F32)
    a_im = a_im.astype(F32)
    dt = jnp.exp(log_dt.astype(F32))[..., None]
    kk = jnp.arange(T + 1, dtype=F32)[:, None, None, None]
    mag = jnp.exp(kk * (a_re * dt)[None])
    pw_re = mag * jnp.cos(kk * (a_im * dt)[None])
    pw_im = mag * jnp.sin(kk * (a_im * dt)[None])

    def cmul(xr, xi, yr, yi):
        return xr * yr - xi * yi, xr * yi + xi * yr

    nr, ni = pw_re[1] - 1.0, pw_im[1]
    den = a_re * a_re + a_im * a_im
    zr, zi = (nr * a_re + ni * a_im) / den, (ni * a_re - nr * a_im) / den
    bb_re, bb_im = cmul(zr[..., None], zi[..., None], b_re.astype(F32), b_im.astype(F32))
    c_re = c_re.astype(F32)
    c_im = c_im.astype(F32)
    G, P = c_re.shape[1], c_re.shape[2]
    e_re, e_im = cmul(c_re[None], c_im[None], pw_re[:T, :, :, None, :], pw_im[:T, :, :, None, :])
    kern = (jnp.einsum('kdgpn,dgnq->dgqpk', e_re, bb_re, precision=hp)
            - jnp.einsum('kdgpn,dgnq->dgqpk', e_im, bb_im, precision=hp))
    kflat = jnp.stack([kern[0], kern[1][..., ::-1]], axis=1).reshape(G, 2, P, P * T)
    s_idx = jnp.arange(T)
    def pack_s(pr, pi, d):
        zr_, zi_ = cmul(pr[..., None], pi[..., None], bb_re[d][None], bb_im[d][None])
        f = lambda z: z.transpose(1, 3, 0, 2).reshape(G, P * T, -1)
        return f(zr_), f(zi_)

    sfr, sfi = pack_s(pw_re[T - 1 - s_idx, 0], pw_im[T - 1 - s_idx, 0], 0)
    sbr, sbi = pack_s(pw_re[s_idx, 1], pw_im[s_idx, 1], 1)
    sall = jnp.concatenate([sfr, sfi, sfi, sfr, sbr, sbi, sbi, sbr], axis=-1)
    def pack_w(pr, pi, d):
        zr_, zi_ = cmul(c_re[d][:, :, :, None], c_im[d][:, :, :, None],
                        pr.transpose(1, 2, 0)[:, None], pi.transpose(1, 2, 0)[:, None])
        f = lambda z: z.transpose(0, 2, 1, 3).reshape(G, -1, P * T)
        return jnp.concatenate([f(zr_), -f(zi_)], axis=1)

    wst = jnp.concatenate([pack_w(pw_re[1 + s_idx, 0], pw_im[1 + s_idx, 0], 0),
                           pack_w(pw_re[T - s_idx, 1], pw_im[T - s_idx, 1], 1)], axis=1)
    rows = []
    for dd in range(2):
        ar, ai = pw_re[T, dd], pw_im[T, dd]
        rows += [jnp.concatenate([ar, ar], -1), jnp.concatenate([-ai, ai], -1), jnp.concatenate([ai, -ai], -1)]
    rows += [jnp.zeros_like(rows[0])] * 2
    dec = jnp.stack(rows, axis=1)
    return kflat.astype(F32), sall.astype(BF16), wst.astype(BF16), dec.astype(F32)


def _s5_kernel(n_ctx_chunks, n_lat_chunks, nb, a_ref, kflat_ref, sall_ref, wst_ref, dec_ref,
               o_ref, x_ref, h_ref, toe_ref):
    T = SSM_T
    kf = kflat_ref[0, 0]
    kb = kflat_ref[0, 1]
    w = kf.shape[1]
    tpos = lax.broadcasted_iota(I32, kf.shape, 1) % T
    for s in range(T):
        row = (jnp.where(tpos >= s, pltpu.roll(kf, s, axis=1), 0.0)
               + jnp.where(tpos <= s, pltpu.roll(kb, (w - (T - 1 - s)) % w, axis=1), 0.0))
        for c in range(w // LANES):
            toe_ref[c, pl.ds(s, SSM_CH, stride=T), :] = row[:, c * LANES:(c + 1) * LANES]
    a = a_ref[0].astype(BF16)
    y = jnp.concatenate([_dot(a, toe_ref[c].astype(BF16)) for c in range(w // LANES)], axis=1)
    x = _dot(a, sall_ref[0])
    for i in range(4):
        x_ref[i] = x[:, i * LANES:(i + 1) * LANES]
    dec = dec_ref[0]
    n_chunks = n_ctx_chunks + n_lat_chunks
    fwd_order = list(range(n_lat_chunks, n_chunks)) + list(range(n_lat_chunks))
    bwd_order = list(range(n_chunks - 1, n_lat_chunks - 1, -1)) + list(range(n_lat_chunks - 1, -1, -1))
    for d, order in enumerate((fwd_order, bwd_order)):
        a1 = dec[3 * d:3 * d + 1]
        a2 = dec[3 * d + 1:3 * d + 2]
        a3 = dec[3 * d + 2:3 * d + 3]
        s1 = jnp.zeros((nb, LANES), F32)
        s2 = jnp.zeros((nb, LANES), F32)
        for c in order:
            if c < n_lat_chunks:
                r = pl.ds(c, nb, stride=n_lat_chunks)
            else:
                r = pl.ds(nb * n_lat_chunks + c - n_lat_chunks, nb, stride=n_ctx_chunks)
            h_ref[d, r, :] = s1
            x1 = x_ref[2 * d, r, :]
            x2 = x_ref[2 * d + 1, r, :]
            s1, s2 = a1 * s1 + a2 * s2 + x1, a1 * s2 + a3 * s1 + x2
    for d in range(2):
        y = y + _dot(h_ref[d].astype(BF16), wst_ref[0, d * LANES:(d + 1) * LANES, :])
    o_ref[0] = y


def s5_core(a_g, kflat, sall, wst, dec, n_ctx_chunks, n_lat_chunks, nb):
    G, rows, w = a_g.shape
    return pl.pallas_call(
        functools.partial(_s5_kernel, n_ctx_chunks, n_lat_chunks, nb),
        grid=(G,),
        in_specs=[
            pl.BlockSpec((1, rows, w), lambda g: (g, 0, 0)),
            pl.BlockSpec((1, 2, SSM_CH, w), lambda g: (g, 0, 0, 0)),
            pl.BlockSpec((1, w, 512), lambda g: (g, 0, 0)),
            pl.BlockSpec((1, 256, w), lambda g: (g, 0, 0)),
            pl.BlockSpec((1, 8, LANES), lambda g: (g, 0, 0)),
        ],
        out_specs=pl.BlockSpec((1, rows, w), lambda g: (g, 0, 0)),
        out_shape=jax.ShapeDtypeStruct((G, rows, w), F32),
        scratch_shapes=[pltpu.VMEM((4, rows, LANES), F32), pltpu.VMEM((2, rows, LANES), F32),
                        pltpu.VMEM((w // LANES, w, LANES), F32)],
        compiler_params=_cparams(("arbitrary",)),
        name="s5_core",
    )(a_g, kflat, sall, wst, dec)


def s5_to_groups(u):
    T = SSM_T
    G = u.shape[1] // SSM_CH
    return u.reshape(-1, T, G, SSM_CH).transpose(2, 0, 3, 1).reshape(G, -1, SSM_CH * T)


def s5_from_groups(y):
    T = SSM_T
    G = y.shape[0]
    return y.reshape(G, -1, SSM_CH, T).transpose(1, 3, 0, 2).reshape(-1, G * SSM_CH)


def _s5_post_kernel(y_ref, u_ref, d_ref, w_ref, o_ref):
    y = jax.nn.gelu(y_ref[...] + d_ref[...] * u_ref[...])
    z = _dot(y.astype(BF16), w_ref[...])
    w = z.shape[1] // 2
    o_ref[...] = (z[:, :w] * jax.nn.sigmoid(z[:, w:])).astype(BF16)


def s5_post(y_nat, pu, d, w_glu_bf, n_rows):
    nt = n_rows
    return pl.pallas_call(
        _s5_post_kernel,
        grid=(nt // TM,),
        in_specs=[
            pl.BlockSpec((TM, 256), lambda j: (j, 0)),
            pl.BlockSpec((TM, 256), lambda j: (j, 0)),
            pl.BlockSpec((1, 256), lambda j: (0, 0)),
            pl.BlockSpec((256, 512), lambda j: (0, 0)),
        ],
        out_specs=pl.BlockSpec((TM, 256), lambda j: (j, 0)),
        out_shape=jax.ShapeDtypeStruct((nt, 256), BF16),
        compiler_params=_cparams(("arbitrary",)),
        name="s5_post",
    )(y_nat, pu, d, w_glu_bf)


def na_bias_table(rpb):
    ci = jnp.arange(GRID_W)
    cs = jnp.clip(ci - WIN_COLS // 2, 0, GRID_W - WIN_COLS)
    kc = jnp.arange(GRID_W)
    inside = (kc[None, :] >= cs[:, None]) & (kc[None, :] < cs[:, None] + WIN_COLS)
    col_off = kc[None, :] - ci[:, None] + (WIN_COLS - 1)
    sel = (col_off[:, :, None] == jnp.arange(2 * WIN_COLS - 1)[None, None, :]).astype(F32)
    base = jnp.einsum('hrw,qkw->hrqk', rpb.astype(F32), sel, precision=lax.Precision.HIGHEST)
    base = jnp.where(inside[None, None], base, NEG)
    tab = jnp.stack([base[:, WIN_ROWS - 1 - ty:2 * WIN_ROWS - 1 - ty] for ty in range(WIN_ROWS)], axis=0)
    tab = tab.transpose(0, 1, 3, 2, 4)
    return tab.reshape(WIN_ROWS, NA_HEADS * GRID_W, WIN_ROWS * GRID_W)


def _lane_blocks(x, op):
    out = x[:, 0:LANES]
    for i in range(1, x.shape[1] // LANES):
        out = op(out, x[:, i * LANES:(i + 1) * LANES])
    return out


def _heads_stack(q):
    lane = lax.broadcasted_iota(I32, q.shape, 1)
    zero = jnp.zeros_like(q)
    return jnp.concatenate([jnp.where(lane // HEAD_DIM == hh, q, zero) for hh in range(NA_HEADS)], axis=0)


def _heads_merge(o4, m):
    lane = lax.broadcasted_iota(I32, (m, o4.shape[1]), 1)
    out = jnp.zeros((m, o4.shape[1]), F32)
    for hh in range(NA_HEADS):
        out = jnp.where(lane // HEAD_DIM == hh, o4[hh * m:(hh + 1) * m], out)
    return out


def _na_kernel(rows_per_step, n_rows, lat_steps, q_ref, k_ref, v_ref, kc_ref, vc_ref, tab_ref, o_ref):
    i = pl.program_id(1)
    kc = kc_ref[...]
    vc = vc_ref[...]
    nkeys = WIN_ROWS * GRID_W

    @pl.when(i < lat_steps)
    def _():
        def body(a, carry):
            r = i * rows_per_step + a
            rs = jnp.clip(r - WIN_ROWS // 2, 0, n_rows - WIN_ROWS)
            ty = r - rs
            q0 = pl.multiple_of(a * GRID_W, GRID_W)
            k0 = pl.multiple_of(rs * GRID_W, GRID_W)
            q4 = _heads_stack(q_ref[pl.ds(q0, GRID_W), :])
            kw = k_ref[pl.ds(k0, nkeys), :]
            vw = v_ref[pl.ds(k0, nkeys), :]
            s = _dot_nt(q4, kw) + tab_ref[ty]
            sc = _dot_nt(q4, kc)
            m = jnp.max(jnp.maximum(_lane_blocks(s, jnp.maximum), _lane_blocks(sc, jnp.maximum)),
                        axis=-1, keepdims=True)
            e = jnp.exp(s - m)
            ec = jnp.exp(sc - m)
            den = jnp.sum(_lane_blocks(e, jnp.add) + _lane_blocks(ec, jnp.add), axis=-1, keepdims=True)
            o4 = (_dot(e.astype(BF16), vw) + _dot(ec.astype(BF16), vc)) / den
            o_ref[pl.ds(q0, GRID_W), :] = _heads_merge(o4, GRID_W).astype(BF16)
            return carry

        lax.fori_loop(0, rows_per_step, body, 0, unroll=True)

    @pl.when(i >= lat_steps)
    def _():
        q4 = _heads_stack(q_ref[...])
        s = _dot_nt(q4, kc)
        m = jnp.max(_lane_blocks(s, jnp.maximum), axis=-1, keepdims=True)
        e = jnp.exp(s - m)
        den = jnp.sum(_lane_blocks(e, jnp.add), axis=-1, keepdims=True)
        o4 = _dot(e.astype(BF16), vc) / den
        o_ref[...] = _heads_merge(o4, q_ref.shape[0]).astype(BF16)


def na_attention(q, k, v, tab, nb, seq_len, ctx_len, with_ctx):
    tq = ctx_len
    rows_per_step = tq // GRID_W
    n_rows = seq_len // GRID_W
    lat_steps = seq_len // tq
    ctx_blk0 = nb * seq_len // ctx_len

    def q_blk(b, i):
        return (jnp.where(i < lat_steps, b * lat_steps + i, ctx_blk0 + b), 0)

    n_out = nb * seq_len + (nb * ctx_len if with_ctx else 0)
    return pl.pallas_call(
        functools.partial(_na_kernel, rows_per_step, n_rows, lat_steps),
        grid=(nb, lat_steps + (1 if with_ctx else 0)),
        in_specs=[
            pl.BlockSpec((tq, 256), q_blk),
            pl.BlockSpec((seq_len, 256), lambda b, i: (b, 0)),
            pl.BlockSpec((seq_len, 256), lambda b, i: (b, 0)),
            pl.BlockSpec((ctx_len, 256), lambda b, i: (ctx_blk0 + b, 0)),
            pl.BlockSpec((ctx_len, 256), lambda b, i: (ctx_blk0 + b, 0)),
            pl.BlockSpec(tab.shape, lambda b, i: (0, 0, 0)),
        ],
        out_specs=pl.BlockSpec((tq, 256), q_blk),
        out_shape=jax.ShapeDtypeStruct((n_out, 256), BF16),
        compiler_params=_cparams(("arbitrary", "arbitrary")),
        name="na_attention",
    )(q, k, v, k, v, tab)


def _merge_kernel(tiles_per_seq, n_lat_tiles, x_ref, mod_ref, gpre_ref, gpost_ref, gffn_ref,
                  ya_ref, yb_ref, yg_ref, yd_ref, wbg_ref, bbg_ref, wbr_ref, wout_ref,
                  wrh_ref, wrl_ref, br_ref, tri_ref, xn_ref, hf_ref, ti_ref, rk_ref, tw_ref, cnt_ref,
                  run_ref):
    x = x_ref[...]
    d = x.shape[1]
    mod = mod_ref[0]
    h = _rms(x, gpre_ref[...]) * (1.0 + mod[1:2]) + mod[0:1]
    hb = h.astype(BF16)
    acc = jnp.zeros_like(x)
    for bi, br_ref_i in enumerate((ya_ref, yb_ref, yg_ref, yd_ref)):
        gate = jax.nn.sigmoid(_dot(hb, wbg_ref[:, bi * d:(bi + 1) * d]) + bbg_ref[:, bi * d:(bi + 1) * d])
        acc = acc + gate * _dot(br_ref_i[...], wbr_ref[bi])
    y = _dot(acc.astype(BF16), wout_ref[...])
    xn = x + mod[2:3] * _rms(y, gpost_ref[...])
    xn_ref[...] = xn
    hf = _rms(xn, gffn_ref[...]) * (1.0 + mod[4:5]) + mod[3:4]
    hf_hi = hf.astype(BF16)
    bits = lax.bitcast_convert_type(hf_hi.astype(F32), jnp.uint32)
    hf_ref[...] = (bits[:, :d // 2] >> 16) | (bits[:, d // 2:] & jnp.uint32(0xFFFF0000))
    hf_lo = (hf - hf_hi.astype(F32)).astype(BF16)
    logits = _dot(hf_hi, wrh_ref[...]) + (_dot(hf_hi, wrl_ref[...]) + _dot(hf_lo, wrh_ref[...])) + br_ref[...]

    @pl.when(pl.program_id(0) == 0)
    def _():
        run_ref[...] = jnp.zeros_like(run_ref)

    ne = run_ref.shape[0]
    tm = logits.shape[0]
    work = jnp.transpose(logits)[:ne]
    erow = lax.broadcasted_iota(I32, work.shape, 0)
    idxs, es, ohs = [], [], []
    v0 = None
    for kk in range(TOP_K):
        m = jnp.max(work, axis=0, keepdims=True)
        idx = jnp.min(jnp.where(work == m, erow, ne), axis=0, keepdims=True)
        oh = erow == idx
        if kk == 0:
            v0 = m
            es.append(jnp.ones_like(m))
        else:
            es.append(jnp.exp(m - v0))
        idxs.append(idx)
        ohs.append(oh)
        work = jnp.where(oh, -jnp.inf, work)
    den = es[0] + es[1] + es[2] + es[3]
    ohf = [o.astype(F32) for o in ohs]
    cnt = _dot(jnp.concatenate([o.astype(BF16) for o in ohf], axis=0), tri_ref[...])
    base = run_ref[:, 0:1]
    r8 = lax.broadcasted_iota(I32, (8, tm), 0)
    r128 = lax.broadcasted_iota(I32, (LANES, tm), 0)
    ti8 = jnp.zeros((8, tm), I32)
    rk8 = jnp.zeros((8, tm), I32)
    twt = jnp.zeros((LANES, tm), F32)
    for kk in range(TOP_K):
        rank = jnp.sum(ohf[kk] * (base + cnt[kk * ne:(kk + 1) * ne]), axis=0, keepdims=True)
        base = base + jnp.sum(ohf[kk], axis=1, keepdims=True)
        ti8 = jnp.where(r8 == kk, idxs[kk], ti8)
        rk8 = jnp.where(r8 == kk, rank.astype(I32), rk8)
        twt = jnp.where(r128 == kk, es[kk] / den, twt)
    run_ref[...] = jnp.broadcast_to(base, run_ref.shape)
    ti_ref[...] = ti8
    rk_ref[...] = rk8
    tw_ref[...] = jnp.transpose(twt)
    cnt_ref[...] = run_ref[...]


def merge_and_route(x, mod, gpre, gpost, gffn, ya, yb, yg, yd, wbg, bbg, wbr, wout, wrh, wrl, br,
                    n_tokens, n_lat, seq_len, n_experts):
    d = x.shape[1]
    n_lat_tiles = n_lat // TM
    tiles_per_seq = seq_len // TM

    def mod_row(j):
        return jnp.where(j < n_lat_tiles, j // tiles_per_seq, 4)

    tri = jnp.triu(jnp.ones((TM, TM), F32), 1).astype(BF16)
    tok = lambda w: pl.BlockSpec((TM, w), lambda j: (j, 0))
    lanes = pl.BlockSpec((8, TM), lambda j: (0, j))
    full = lambda shape: pl.BlockSpec(shape, lambda j: tuple(0 for _ in shape))
    return pl.pallas_call(
        functools.partial(_merge_kernel, tiles_per_seq, n_lat_tiles),
        grid=(n_tokens // TM,),
        in_specs=[
            tok(d),
            pl.BlockSpec((1, 6, d), lambda j: (mod_row(j), 0, 0)),
            full((1, d)), full((1, d)), full((1, d)),
            tok(256), tok(256), tok(256), tok(256),
            full(wbg.shape), full(bbg.shape), full(wbr.shape), full(wout.shape),
            full(wrh.shape), full(wrl.shape), full(br.shape), full(tri.shape),
        ],
        out_specs=[tok(d), tok(d // 2), lanes, lanes, tok(LANES), full((n_experts, LANES))],
        out_shape=[
            jax.ShapeDtypeStruct((n_tokens, d), F32),
            jax.ShapeDtypeStruct((n_tokens, d // 2), jnp.uint32),
            jax.ShapeDtypeStruct((8, n_tokens), I32),
            jax.ShapeDtypeStruct((8, n_tokens), I32),
            jax.ShapeDtypeStruct((n_tokens, LANES), F32),
            jax.ShapeDtypeStruct((n_experts, LANES), F32),
        ],
        scratch_shapes=[pltpu.VMEM((n_experts, LANES), F32)],
        compiler_params=_cparams(("arbitrary",)),
        name="merge_and_route",
    )(x, mod, gpre, gpost, gffn, ya, yb, yg, yd, wbg, bbg, wbr, wout, wrh, wrl, br, tri)


def _expert_kernel(be_ref, nv_ref, x_ref, w1_ref, b1_ref, w2_ref, b2_ref, o_ref, w1b_ref, w2b_ref):
    i = pl.program_id(0)
    prev = be_ref[jnp.maximum(i - 1, 0)]
    changed = jnp.logical_or(i == 0, be_ref[i] != prev)

    @pl.when(jnp.logical_and(i < nv_ref[0], changed))
    def _():
        w1b_ref[...] = w1_ref[0].astype(BF16)
        w2b_ref[...] = w2_ref[0].astype(BF16)

    @pl.when(i < nv_ref[0])
    def _():
        w = x_ref[...]
        half = w.shape[1]
        x_lo = lax.bitcast_convert_type(w << 16, F32).astype(BF16)
        x_hi = lax.bitcast_convert_type(w & jnp.uint32(0xFFFF0000), F32).astype(BF16)
        h = _dot(x_lo, w1b_ref[0:half, :]) + _dot(x_hi, w1b_ref[half:, :]) + b1_ref[0]
        de = h.shape[1] // 2
        g = jnp.minimum(h[:, :de], SWIGLU_LIMIT)
        lin = jnp.clip(h[:, de:], -SWIGLU_LIMIT, SWIGLU_LIMIT)
        act = g * jax.nn.sigmoid(SWIGLU_ALPHA * g) * (lin + 1.0)
        y = _dot(act.astype(BF16), w2b_ref[...]) + b2_ref[0]
        bits = lax.bitcast_convert_type(y.astype(BF16).astype(F32), jnp.uint32)
        o_ref[...] = (bits[:, :half] >> 16) | (bits[:, half:] & jnp.uint32(0xFFFF0000))

    @pl.when(i >= nv_ref[0])
    def _():
        o_ref[...] = jnp.zeros_like(o_ref)


def moe_experts(xb, block_e, n_valid, w1, b1, w2, b2, layer):
    p = xb.shape[0]
    depth, ne, d, de2 = w1.shape
    n_blocks = p // MOE_BM
    grid_spec = pltpu.PrefetchScalarGridSpec(
        num_scalar_prefetch=2,
        grid=(n_blocks,),
        in_specs=[
            pl.BlockSpec((MOE_BM, d // 2), lambda i, be, nv: (i, 0)),
            pl.BlockSpec((None, 1, d, de2), lambda i, be, nv: (layer, be[i], 0, 0)),
            pl.BlockSpec((None, 1, 1, de2), lambda i, be, nv: (layer, be[i], 0, 0)),
            pl.BlockSpec((None, 1, de2 // 2, d), lambda i, be, nv: (layer, be[i], 0, 0)),
            pl.BlockSpec((None, 1, 1, d), lambda i, be, nv: (layer, be[i], 0, 0)),
        ],
        out_specs=pl.BlockSpec((MOE_BM, d // 2), lambda i, be, nv: (i, 0)),
        scratch_shapes=[pltpu.VMEM((d, de2), BF16), pltpu.VMEM((de2 // 2, d), BF16)],
    )
    return pl.pallas_call(
        _expert_kernel,
        grid_spec=grid_spec,
        out_shape=jax.ShapeDtypeStruct((p, d // 2), jnp.uint32),
        compiler_params=_cparams(("arbitrary",)),
        name="moe_experts",
    )(block_e, n_valid, xb, w1, b1.reshape(depth, ne, 1, de2), w2, b2.reshape(depth, ne, 1, d))


def _combine_kernel(dcur_ref, dnext_ref, x_ref, mod_ref, g_ref, tw_ref, yblk_ref, o_ref, gbuf, sem):
    j = pl.program_id(0)
    n = pl.num_programs(0)
    slot = j % 2
    tm = x_ref.shape[0]

    def row_copy(dref, s, t, kk):
        return pltpu.make_async_copy(yblk_ref.at[pl.ds(dref[kk * tm + t], 1)], gbuf.at[s, kk, pl.ds(t, 1)],
                                     sem.at[s])

    def issue(dref, s):
        def body(i, carry):
            t0 = pl.multiple_of(i * DMA_UNROLL, DMA_UNROLL)
            for u in range(DMA_UNROLL):
                for kk in range(TOP_K):
                    row_copy(dref, s, t0 + u, kk).start()
            return carry
        lax.fori_loop(0, tm // DMA_UNROLL, body, 0)

    @pl.when(j == 0)
    def _():
        issue(dcur_ref, slot)

    @pl.when(j + 1 < n)
    def _():
        issue(dnext_ref, 1 - slot)

    def wait(i, carry):
        t0 = pl.multiple_of(i * DMA_UNROLL, DMA_UNROLL)
        for u in range(DMA_UNROLL):
            for kk in range(TOP_K):
                row_copy(dcur_ref, slot, t0 + u, kk).wait()
        return carry
    lax.fori_loop(0, tm // DMA_UNROLL, wait, 0)

    tw = tw_ref[...]
    y_lo = None
    y_hi = None
    for kk in range(TOP_K):
        w = gbuf[slot, kk]
        lo = tw[:, kk:kk + 1] * lax.bitcast_convert_type(w << 16, F32)
        hi = tw[:, kk:kk + 1] * lax.bitcast_convert_type(w & jnp.uint32(0xFFFF0000), F32)
        y_lo = lo if y_lo is None else y_lo + lo
        y_hi = hi if y_hi is None else y_hi + hi
    y = jnp.concatenate([y_lo, y_hi], axis=1)
    o_ref[...] = x_ref[...] + mod_ref[0, 5:6, :] * _rms(y, g_ref[...])


def moe_combine(dest_t, xn, mod, gain, tw, yblk, n_tokens, n_lat, seq_len):
    d = xn.shape[1]
    n_lat_tiles = n_lat // TM
    tiles_per_seq = seq_len // TM
    n_tiles = n_tokens // TM

    def mod_row(j):
        return jnp.where(j < n_lat_tiles, j // tiles_per_seq, 4)

    return pl.pallas_call(
        _combine_kernel,
        grid=(n_tiles,),
        in_specs=[
            pl.BlockSpec((8 * TM,), lambda j: (j,), memory_space=pltpu.SMEM),
            pl.BlockSpec((8 * TM,), lambda j: (jnp.minimum(j + 1, n_tiles - 1),), memory_space=pltpu.SMEM),
            pl.BlockSpec((TM, d), lambda j: (j, 0)),
            pl.BlockSpec((1, 6, d), lambda j: (mod_row(j), 0, 0)),
            pl.BlockSpec((1, d), lambda j: (0, 0)),
            pl.BlockSpec((TM, LANES), lambda j: (j, 0)),
            pl.BlockSpec(memory_space=pl.ANY),
        ],
        out_specs=pl.BlockSpec((TM, d), lambda j: (j, 0)),
        out_shape=jax.ShapeDtypeStruct((n_tokens, d), F32),
        scratch_shapes=[pltpu.VMEM((2, TOP_K, TM, d // 2), jnp.uint32), pltpu.SemaphoreType.DMA((2,))],
        compiler_params=_cparams(("arbitrary",)),
        name="moe_combine",
    )(dest_t, dest_t, xn, mod, gain, tw, yblk)


def _dest_kernel(n_experts, ps_ref, ti_ref, rk_ref, o_ref):
    ti = ti_ref[...]
    dest = rk_ref[...]
    for e in range(n_experts):
        dest = dest + jnp.where(ti == e, ps_ref[e], 0)
    o_ref[0] = dest


def moe_dest(pstart, ti_t, rk_t):
    n_tokens = ti_t.shape[1]
    lanes = pl.BlockSpec((8, TM), lambda j, ps: (0, j))
    dest = pl.pallas_call(
        functools.partial(_dest_kernel, pstart.shape[0]),
        grid_spec=pltpu.PrefetchScalarGridSpec(
            num_scalar_prefetch=1, grid=(n_tokens // TM,), in_specs=[lanes, lanes],
            out_specs=pl.BlockSpec((1, 8, TM), lambda j, ps: (j, 0, 0))),
        out_shape=jax.ShapeDtypeStruct((n_tokens // TM, 8, TM), I32),
        compiler_params=_cparams(("arbitrary",)),
        name="moe_dest",
    )(pstart, ti_t, rk_t)
    return dest.reshape(-1)


def _dispatch_kernel(dest_ref, hf_ref, xb_in_ref, xb_ref, sem):
    del xb_in_ref
    tm = hf_ref.shape[0]

    def row_copy(t, kk):
        return pltpu.make_async_copy(hf_ref.at[pl.ds(t, 1)], xb_ref.at[pl.ds(dest_ref[kk * tm + t], 1)], sem)

    def start(i, carry):
        t0 = pl.multiple_of(i * DMA_UNROLL, DMA_UNROLL)
        for u in range(DMA_UNROLL):
            for kk in range(TOP_K):
                row_copy(t0 + u, kk).start()
        return carry
    lax.fori_loop(0, tm // DMA_UNROLL, start, 0)

    def wait(i, carry):
        t0 = pl.multiple_of(i * DMA_UNROLL, DMA_UNROLL)
        for u in range(DMA_UNROLL):
            for kk in range(TOP_K):
                row_copy(t0 + u, kk).wait()
        return carry
    lax.fori_loop(0, tm // DMA_UNROLL, wait, 0)


def moe_dispatch(dest_t, hfp, n_rows):
    n_tokens, w = hfp.shape
    xb0 = jnp.zeros((n_rows, w), hfp.dtype)
    return pl.pallas_call(
        _dispatch_kernel,
        grid=(n_tokens // TM,),
        in_specs=[
            pl.BlockSpec((8 * TM,), lambda j: (j,), memory_space=pltpu.SMEM),
            pl.BlockSpec((TM, w), lambda j: (j, 0)),
            pl.BlockSpec(memory_space=pl.ANY),
        ],
        out_specs=pl.BlockSpec(memory_space=pl.ANY),
        out_shape=jax.ShapeDtypeStruct((n_rows, w), hfp.dtype),
        input_output_aliases={2: 0},
        scratch_shapes=[pltpu.SemaphoreType.DMA(())],
        compiler_params=_cparams(("arbitrary",)),
        name="moe_dispatch",
    )(dest_t, hfp, xb0)


def kernel(x, c, ctx, c_ctx, ada_w, ada_b, norm_mix_pre, norm_mix_post, norm_ffn_pre, norm_ffn_post,
           w_in, w_branch_gate, b_branch_gate, w_branch, w_out, pool_w, pool_scale,
           ssm_a_re, ssm_a_im, ssm_log_dt, ssm_b_re, ssm_b_im, ssm_c_re, ssm_c_im, ssm_d, ssm_w_glu,
           gmlp_ln_g, gmlp_ln_b, gmlp_w_s, gmlp_b_s, na_rpb,
           router_w, router_b, expert_w1, expert_b1, expert_w2, expert_b2):
    nb, seq_len, d = x.shape
    ctx_len = ctx.shape[1]
    depth = ada_w.shape[0]
    n_lat = nb * seq_len
    n_all = n_lat + nb * ctx_len
    n_experts = router_w.shape[-1]
    assert nb == 4 and seq_len % TM == 0 and (nb * ctx_len) % TM == 0 and ctx_len % TP == 0

    xs = jnp.concatenate([x.reshape(n_lat, d), ctx.reshape(nb * ctx_len, d)], axis=0)
    cvec = jnp.concatenate([c, c_ctx[None, :], jnp.zeros((3, d), F32)], axis=0)
    mods = ada_modulation(cvec, ada_w, ada_b).reshape(depth, 8, 6, d)
    cos_t, sin_t = rope_tables(seq_len)

    out = None
    for l in range(depth):
        last = l == depth - 1
        mod = mods[l]
        row = lambda v: v.reshape(1, -1)
        pa, pu, pg, q, k, v = premix(xs, mod, row(norm_mix_pre[l]), w_in[l].astype(BF16), cos_t, sin_t,
                                     n_lat, seq_len)

        n_mix = n_lat if last else n_all
        w_bd = jax.scipy.linalg.block_diag(*[pool_w[l, g] for g in range(POOL_GROUPS)]).astype(BF16)
        ya = pool_mix(pa, w_bd, row(pool_scale[l]), n_mix, n_lat, seq_len, ctx_len)

        bs_full = jnp.repeat(gmlp_b_s[l].T, 256 // GMLP_HEADS, axis=1)
        yg = gmlp_mix(pg, row(gmlp_ln_g[l]), row(gmlp_ln_b[l]), gmlp_w_s[l].astype(BF16), bs_full, n_mix)

        kflat, sall, wst, dec = s5_tables(ssm_a_re[l], ssm_a_im[l], ssm_log_dt[l], ssm_b_re[l], ssm_b_im[l],
                                          ssm_c_re[l], ssm_c_im[l])
        y_g = s5_core(s5_to_groups(pu), kflat, sall, wst, dec, ctx_len // SSM_T, seq_len // SSM_T, nb)
        yb = s5_post(s5_from_groups(y_g), pu, row(ssm_d[l]), ssm_w_glu[l].astype(BF16), n_mix)

        yd = na_attention(q, k, v, na_bias_table(na_rpb[l]), nb, seq_len, ctx_len, not last)

        rw = jnp.pad(router_w[l], ((0, 0), (0, LANES - n_experts)))
        rw_hi = rw.astype(BF16)
        rw_lo = (rw - rw_hi.astype(F32)).astype(BF16)
        rb = jnp.pad(router_b[l], (0, LANES - n_experts), constant_values=NEG).reshape(1, LANES)
        n_tok = n_lat if last else n_all
        xn, hfp, ti_t, rk_t, tw, counts = merge_and_route(
            xs, mod, row(norm_mix_pre[l]), row(norm_mix_post[l]), row(norm_ffn_pre[l]),
            ya, yb, yg, yd, w_branch_gate[l].astype(BF16), row(b_branch_gate[l]),
            w_branch[l].astype(BF16), w_out[l].astype(BF16), rw_hi, rw_lo, rb, n_tok, n_lat, seq_len, n_experts)

        cnt = counts[:, 0].astype(I32)
        padded = (cnt + MOE_BM - 1) // MOE_BM * MOE_BM
        pend = jnp.cumsum(padded)
        n_blocks = n_tok * TOP_K // MOE_BM + n_experts
        blk_start = jnp.arange(n_blocks, dtype=I32) * MOE_BM
        block_e = jnp.minimum(jnp.sum((pend[None, :] <= blk_start[:, None]).astype(I32), axis=1), n_experts - 1)
        n_valid = (pend[-1] // MOE_BM).astype(I32).reshape(1)

        dest_t = moe_dest(pend - padded, ti_t, rk_t)
        xb = moe_dispatch(dest_t, hfp, n_blocks * MOE_BM)
        yblk = moe_experts(xb, block_e, n_valid, expert_w1, expert_b1, expert_w2, expert_b2, l)
        xs = moe_combine(dest_t, xn, mod, row(norm_ffn_post[l]), tw, yblk, n_tok, n_lat, seq_len)
        out = xs
    return out.reshape(nb, seq_len, d)
```

```python
import functools
import math

import jax
import jax.numpy as jnp
import jax.scipy.linalg
from jax import lax
from jax.experimental import pallas as pl
from jax.experimental.pallas import tpu as pltpu

F32 = jnp.float32
BF16 = jnp.bfloat16
I32 = jnp.int32

GRID_W = 64
EPS = 1e-6
POOL_GROUPS = 4
POOL_WINDOWS = (2, 4, 8, 16)
SSM_CH = 16
SSM_N = 64
GMLP_HEADS = 4
CHUNK = 128
NA_HEADS = 4
HEAD_DIM = 64
WIN_ROWS = 8
WIN_COLS = 16
ROPE_BASE = 10000.0
BRANCH_W = 256
N_BRANCH = 4
TOP_K = 4
SWIGLU_LIMIT = 7.0
SWIGLU_ALPHA = 1.702

TM = 512
TP = 256
POOL_HALO = 16
SSM_T = 32
MOE_BM = 256
DMA_UNROLL = 8
SEG = 8
COMBINE_KC = 256
LANES = 128
NEG = -1e30
VMEM_LIMIT = 56 * 1024 * 1024


def _cparams(sem):
    return pltpu.CompilerParams(dimension_semantics=sem, vmem_limit_bytes=VMEM_LIMIT)


def _dot(a, b):
    return jnp.dot(a, b, preferred_element_type=F32)


def _dot_nt(a, b):
    return lax.dot_general(a, b, (((1,), (1,)), ((), ())), preferred_element_type=F32)


def _ada_kernel(c_ref, w_ref, b_ref, o_ref):
    c = c_ref[...]
    s = c * jax.nn.sigmoid(c)
    o_ref[0] = jnp.dot(s, w_ref[0], preferred_element_type=F32,
                       precision=lax.Precision.HIGHEST) + b_ref[0]


def ada_modulation(cvec, ada_w, ada_b):
    depth, d, n = ada_w.shape
    tn = 1536
    return pl.pallas_call(
        _ada_kernel,
        grid=(depth, n // tn),
        in_specs=[
            pl.BlockSpec((8, d), lambda l, j: (0, 0)),
            pl.BlockSpec((1, d, tn), lambda l, j: (l, 0, j)),
            pl.BlockSpec((1, 1, tn), lambda l, j: (l, 0, j)),
        ],
        out_specs=pl.BlockSpec((1, 8, tn), lambda l, j: (l, 0, j)),
        out_shape=jax.ShapeDtypeStruct((depth, 8, n), F32),
        compiler_params=_cparams(("arbitrary", "arbitrary")),
        name="ada_modulation",
    )(cvec, ada_w, ada_b.reshape(depth, 1, n))


def _rms(x, g):
    return x * lax.rsqrt(jnp.mean(x * x, axis=-1, keepdims=True) + EPS) * g


def _premix_kernel(n_lat_tiles, x_ref, mod_ref, g_ref, w_ref, cos_ref, sin_ref,
                   pa_ref, pu_ref, pg_ref, q_ref, k_ref, v_ref):
    j = pl.program_id(0)
    x = x_ref[...]
    h = _rms(x, g_ref[...])
    h = h * (1.0 + mod_ref[0, 1:2, :]) + mod_ref[0, 0:1, :]
    p = _dot(h.astype(BF16), w_ref[...])
    wa = p.shape[1] - 768
    pa_ref[...] = p[:, 0:256]
    pu_ref[...] = p[:, 256:512]
    pg_ref[...] = p[:, 512:wa]
    q = p[:, wa:wa + 256]
    k = p[:, wa + 256:wa + 512]
    v = p[:, wa + 512:wa + 768]
    is_lat = j < n_lat_tiles
    cos = jnp.where(is_lat, cos_ref[...], 1.0)
    sin = jnp.where(is_lat, sin_ref[...], 0.0)
    lane = lax.broadcasted_iota(I32, q.shape, 1)
    first = (lane % 32) < 16

    def rope(t):
        partner = jnp.where(first, pltpu.roll(t, 256 - 16, axis=1), pltpu.roll(t, 16, axis=1))
        return t * cos + partner * sin

    q_ref[...] = (rope(q) * (HEAD_DIM ** -0.5)).astype(BF16)
    k_ref[...] = rope(k).astype(BF16)
    v_ref[...] = v.astype(BF16)


def premix(x, mod, gain, w_in_bf, cos_t, sin_t, n_lat, seq_len):
    nt, d = x.shape
    in_w = w_in_bf.shape[1]
    wa = in_w - 768
    n_lat_tiles = n_lat // TM
    tiles_per_seq = seq_len // TM

    def mod_row(j):
        return jnp.where(j < n_lat_tiles, j // tiles_per_seq, 4)

    def tab_blk(j):
        return jnp.where(j < n_lat_tiles, j % tiles_per_seq, 0)

    return pl.pallas_call(
        functools.partial(_premix_kernel, n_lat_tiles),
        grid=(nt // TM,),
        in_specs=[
            pl.BlockSpec((TM, d), lambda j: (j, 0)),
            pl.BlockSpec((1, 6, d), lambda j: (mod_row(j), 0, 0)),
            pl.BlockSpec((1, d), lambda j: (0, 0)),
            pl.BlockSpec((d, in_w), lambda j: (0, 0)),
            pl.BlockSpec((TM, 256), lambda j: (tab_blk(j), 0)),
            pl.BlockSpec((TM, 256), lambda j: (tab_blk(j), 0)),
        ],
        out_specs=[
            pl.BlockSpec((TM, 256), lambda j: (j, 0)),
            pl.BlockSpec((TM, 256), lambda j: (j, 0)),
            pl.BlockSpec((TM, wa - 512), lambda j: (j, 0)),
            pl.BlockSpec((TM, 256), lambda j: (j, 0)),
            pl.BlockSpec((TM, 256), lambda j: (j, 0)),
            pl.BlockSpec((TM, 256), lambda j: (j, 0)),
        ],
        out_shape=[
            jax.ShapeDtypeStruct((nt, 256), F32),
            jax.ShapeDtypeStruct((nt, 256), F32),
            jax.ShapeDtypeStruct((nt, wa - 512), F32),
            jax.ShapeDtypeStruct((nt, 256), BF16),
            jax.ShapeDtypeStruct((nt, 256), BF16),
            jax.ShapeDtypeStruct((nt, 256), BF16),
        ],
        compiler_params=_cparams(("arbitrary",)),
        name="premix",
    )(x, mod, gain, w_in_bf, cos_t, sin_t)


def rope_tables(seq_len):
    t = jnp.arange(seq_len)
    half = HEAD_DIM // 2
    nf = half // 2
    inv = ROPE_BASE ** (-jnp.arange(nf, dtype=F32) / nf)
    d = jnp.arange(HEAD_DIM)
    pos = jnp.where((d // half)[None, :] == 0, (t // GRID_W)[:, None], (t % GRID_W)[:, None]).astype(F32)
    ang = pos * inv[(d % half) % nf][None, :]
    sign = jnp.where((d % half) < nf, -1.0, 1.0)[None, :]
    cos = jnp.tile(jnp.cos(ang), (1, NA_HEADS))
    sin = jnp.tile(jnp.sin(ang) * sign, (1, NA_HEADS))
    return cos.astype(F32), sin.astype(F32)


def _pool_kernel(n_lat_tiles, tiles_per_seq, ctx_tiles_per_seq, cur_ref, prev_ref, next_ref,
                 w_ref, sc_ref, o_ref, ext_ref):
    j = pl.program_id(0)
    is_lat = j < n_lat_tiles
    t0 = jnp.where(is_lat, (j % tiles_per_seq) * TP, ((j - n_lat_tiles) % ctx_tiles_per_seq) * TP)
    slen = jnp.where(is_lat, tiles_per_seq * TP, ctx_tiles_per_seq * TP)
    u = cur_ref[...]
    h = POOL_HALO
    ext_ref[0:h, :] = jnp.where(t0 > 0, prev_ref[...], 0.0)
    ext_ref[h:h + TP, :] = u
    ext_ref[h + TP:h + TP + h, :] = jnp.where(t0 + TP < slen, next_ref[...], 0.0)
    t = t0 + lax.broadcasted_iota(I32, (TP, 1), 0)
    lane = lax.broadcasted_iota(I32, (TP, u.shape[1]), 1)
    grp = lane // (u.shape[1] // POOL_GROUPS)
    mean = jnp.zeros_like(u)
    for gi, w in enumerate(POOL_WINDOWS):
        acc = ext_ref[pl.ds(h - w // 2, TP), :]
        for o in range(1, w):
            acc = acc + ext_ref[pl.ds(h - w // 2 + o, TP), :]
        cnt = (jnp.minimum(t - w // 2 + w, slen) - jnp.maximum(t - w // 2, 0)).astype(F32)
        mean = jnp.where(grp == gi, acc / cnt, mean)
    y = _dot((mean - u).astype(BF16), w_ref[...]) * sc_ref[...]
    o_ref[...] = y.astype(BF16)


def pool_mix(p1, w_bd_bf, scale, n_rows, n_lat, seq_len, ctx_len):
    nt = n_rows
    hb = TP // POOL_HALO
    n_halo_blocks = p1.shape[0] // POOL_HALO
    return pl.pallas_call(
        functools.partial(_pool_kernel, n_lat // TP, seq_len // TP, ctx_len // TP),
        grid=(nt // TP,),
        in_specs=[
            pl.BlockSpec((TP, 256), lambda j: (j, 0)),
            pl.BlockSpec((POOL_HALO, 256), lambda j: (jnp.maximum(j * hb - 1, 0), 0)),
            pl.BlockSpec((POOL_HALO, 256), lambda j: (jnp.minimum((j + 1) * hb, n_halo_blocks - 1), 0)),
            pl.BlockSpec((256, 256), lambda j: (0, 0)),
            pl.BlockSpec((1, 256), lambda j: (0, 0)),
        ],
        out_specs=pl.BlockSpec((TP, 256), lambda j: (j, 0)),
        out_shape=jax.ShapeDtypeStruct((n_rows, 256), BF16),
        scratch_shapes=[pltpu.VMEM((TP + 2 * POOL_HALO, 256), F32)],
        compiler_params=_cparams(("arbitrary",)),
        name="pool_mix",
    )(p1, p1, p1, w_bd_bf, scale)


def _gmlp_kernel(p_ref, g_ref, b_ref, ws_ref, bs_ref, o_ref):
    uv = jax.nn.gelu(p_ref[...])
    w = uv.shape[1] // 2
    u = uv[:, :w]
    v = uv[:, w:]
    mu = jnp.mean(v, axis=-1, keepdims=True)
    var = jnp.mean((v - mu) ** 2, axis=-1, keepdims=True)
    v = ((v - mu) * lax.rsqrt(var + EPS) * g_ref[...] + b_ref[...]).astype(BF16)
    lane = lax.broadcasted_iota(I32, (CHUNK, w), 1)
    head = lane // (w // GMLP_HEADS)
    for c in range(TP // CHUNK):
        vc = v[c * CHUNK:(c + 1) * CHUNK]
        sv = bs_ref[...]
        for hh in range(GMLP_HEADS):
            sv = sv + jnp.where(head == hh, _dot(ws_ref[hh], vc), 0.0)
        o_ref[c * CHUNK:(c + 1) * CHUNK, :] = (u[c * CHUNK:(c + 1) * CHUNK] * sv).astype(BF16)


def gmlp_mix(p1, ln_g, ln_b, ws_bf, bs_full, n_rows):
    return pl.pallas_call(
        _gmlp_kernel,
        grid=(n_rows // TP,),
        in_specs=[
            pl.BlockSpec((TP, 512), lambda j: (j, 0)),
            pl.BlockSpec((1, 256), lambda j: (0, 0)),
            pl.BlockSpec((1, 256), lambda j: (0, 0)),
            pl.BlockSpec((GMLP_HEADS, CHUNK, CHUNK), lambda j: (0, 0, 0)),
            pl.BlockSpec((CHUNK, 256), lambda j: (0, 0)),
        ],
        out_specs=pl.BlockSpec((TP, 256), lambda j: (j, 0)),
        out_shape=jax.ShapeDtypeStruct((n_rows, 256), BF16),
        compiler_params=_cparams(("arbitrary",)),
        name="gmlp_mix",
    )(p1, ln_g, ln_b, ws_bf, bs_full)


def s5_tables(a_re, a_im, log_dt, b_re, b_im, c_re, c_im):
    hp = lax.Precision.HIGHEST
    T = SSM_T
    a_re = a_re.astype(F32)
    a_im = a_im.astype(F32)
    dt = jnp.exp(log_dt.astype(F32))[..., None]
    kk = jnp.arange(T + 1, dtype=F32)[:, None, None, None]
    mag = jnp.exp(kk * (a_re * dt)[None])
    pw_re = mag * jnp.cos(kk * (a_im * dt)[None])
    pw_im = mag * jnp.sin(kk * (a_im * dt)[None])

    def cmul(xr, xi, yr, yi):
        return xr * yr - xi * yi, xr * yi + xi * yr

    nr, ni = pw_re[1] - 1.0, pw_im[1]
    den = a_re * a_re + a_im * a_im
    zr, zi = (nr * a_re + ni * a_im) / den, (ni * a_re - nr * a_im) / den
    bb_re, bb_im = cmul(zr[..., None], zi[..., None], b_re.astype(F32), b_im.astype(F32))
    c_re = c_re.astype(F32)
    c_im = c_im.astype(F32)
    G, P = c_re.shape[1], c_re.shape[2]
    e_re, e_im = cmul(c_re[None], c_im[None], pw_re[:T, :, :, None, :], pw_im[:T, :, :, None, :])
    kern = (jnp.einsum('kdgpn,dgnq->dgqpk', e_re, bb_re, precision=hp)
            - jnp.einsum('kdgpn,dgnq->dgqpk', e_im, bb_im, precision=hp))
    kflat = jnp.stack([kern[0], kern[1][..., ::-1]], axis=1).reshape(G, 2, P, P * T)
    s_idx = jnp.arange(T)
    def pack_s(pr, pi, d):
        zr_, zi_ = cmul(pr[..., None], pi[..., None], bb_re[d][None], bb_im[d][None])
        f = lambda z: z.transpose(1, 3, 0, 2).reshape(G, P * T, -1)
        return f(zr_), f(zi_)

    sfr, sfi = pack_s(pw_re[T - 1 - s_idx, 0], pw_im[T - 1 - s_idx, 0], 0)
    sbr, sbi = pack_s(pw_re[s_idx, 1], pw_im[s_idx, 1], 1)
    sall = jnp.concatenate([sfr, sfi, sfi, sfr, sbr, sbi, sbi, sbr], axis=-1)
    def pack_w(pr, pi, d):
        zr_, zi_ = cmul(c_re[d][:, :, :, None], c_im[d][:, :, :, None],
                        pr.transpose(1, 2, 0)[:, None], pi.transpose(1, 2, 0)[:, None])
        f = lambda z: z.transpose(0, 2, 1, 3).reshape(G, -1, P * T)
        return jnp.concatenate([f(zr_), -f(zi_)], axis=1)

    wst = jnp.concatenate([pack_w(pw_re[1 + s_idx, 0], pw_im[1 + s_idx, 0], 0),
                           pack_w(pw_re[T - s_idx, 1], pw_im[T - s_idx, 1], 1)], axis=1)
    rows = []
    for dd in range(2):
        ar, ai = pw_re[T, dd], pw_im[T, dd]
        rows += [jnp.concatenate([ar, ar], -1), jnp.concatenate([-ai, ai], -1), jnp.concatenate([ai, -ai], -1)]
    rows += [jnp.zeros_like(rows[0])] * 2
    dec = jnp.stack(rows, axis=1)
    return kflat.astype(F32), sall.astype(BF16), wst.astype(BF16), dec.astype(F32)


def _s5_kernel(n_ctx_chunks, n_lat_chunks, nb, a_ref, kflat_ref, sall_ref, wst_ref, dec_ref,
               o_ref, x_ref, h_ref, toe_ref):
    T = SSM_T
    kf = kflat_ref[0, 0]
    kb = kflat_ref[0, 1]
    w = kf.shape[1]
    tpos = lax.broadcasted_iota(I32, kf.shape, 1) % T
    for s in range(T):
        row = (jnp.where(tpos >= s, pltpu.roll(kf, s, axis=1), 0.0)
               + jnp.where(tpos <= s, pltpu.roll(kb, (w - (T - 1 - s)) % w, axis=1), 0.0))
        for c in range(w // LANES):
            toe_ref[c, pl.ds(s, SSM_CH, stride=T), :] = row[:, c * LANES:(c + 1) * LANES]
    a = a_ref[0].astype(BF16)
    y = jnp.concatenate([_dot(a, toe_ref[c].astype(BF16)) for c in range(w // LANES)], axis=1)
    x = _dot(a, sall_ref[0])
    for i in range(4):
        x_ref[i] = x[:, i * LANES:(i + 1) * LANES]
    dec = dec_ref[0]
    n_chunks = n_ctx_chunks + n_lat_chunks
    fwd_order = list(range(n_lat_chunks, n_chunks)) + list(range(n_lat_chunks))
    bwd_order = list(range(n_chunks - 1, n_lat_chunks - 1, -1)) + list(range(n_lat_chunks - 1, -1, -1))
    for d, order in enumerate((fwd_order, bwd_order)):
        a1 = dec[3 * d:3 * d + 1]
        a2 = dec[3 * d + 1:3 * d + 2]
        a3 = dec[3 * d + 2:3 * d + 3]
        s1 = jnp.zeros((nb, LANES), F32)
        s2 = jnp.zeros((nb, LANES), F32)
        for c in order:
            if c < n_lat_chunks:
                r = pl.ds(c, nb, stride=n_lat_chunks)
            else:
                r = pl.ds(nb * n_lat_chunks + c - n_lat_chunks, nb, stride=n_ctx_chunks)
            h_ref[d, r, :] = s1
            x1 = x_ref[2 * d, r, :]
            x2 = x_ref[2 * d + 1, r, :]
            s1, s2 = a1 * s1 + a2 * s2 + x1, a1 * s2 + a3 * s1 + x2
    for d in range(2):
        y = y + _dot(h_ref[d].astype(BF16), wst_ref[0, d * LANES:(d + 1) * LANES, :])
    o_ref[0] = y


def s5_core(a_g, kflat, sall, wst, dec, layer, n_ctx_chunks, n_lat_chunks, nb):
    G, rows, w = a_g.shape
    return pl.pallas_call(
        functools.partial(_s5_kernel, n_ctx_chunks, n_lat_chunks, nb),
        grid=(G,),
        in_specs=[
            pl.BlockSpec((1, rows, w), lambda g: (g, 0, 0)),
            pl.BlockSpec((None, 1, 2, SSM_CH, w), lambda g: (layer, g, 0, 0, 0)),
            pl.BlockSpec((None, 1, w, 512), lambda g: (layer, g, 0, 0)),
            pl.BlockSpec((None, 1, 256, w), lambda g: (layer, g, 0, 0)),
            pl.BlockSpec((None, 1, 8, LANES), lambda g: (layer, g, 0, 0)),
        ],
        out_specs=pl.BlockSpec((1, rows, w), lambda g: (g, 0, 0)),
        out_shape=jax.ShapeDtypeStruct((G, rows, w), F32),
        scratch_shapes=[pltpu.VMEM((4, rows, LANES), F32), pltpu.VMEM((2, rows, LANES), F32),
                        pltpu.VMEM((w // LANES, w, LANES), F32)],
        compiler_params=_cparams(("arbitrary",)),
        name="s5_core",
    )(a_g, kflat, sall, wst, dec)


def s5_to_groups(u):
    T = SSM_T
    G = u.shape[1] // SSM_CH
    return u.reshape(-1, T, G, SSM_CH).transpose(2, 0, 3, 1).reshape(G, -1, SSM_CH * T)


def s5_from_groups(y):
    T = SSM_T
    G = y.shape[0]
    return y.reshape(G, -1, SSM_CH, T).transpose(1, 3, 0, 2).reshape(-1, G * SSM_CH)


def _s5_post_kernel(y_ref, u_ref, d_ref, w_ref, o_ref):
    y = jax.nn.gelu(y_ref[...] + d_ref[...] * u_ref[...])
    z = _dot(y.astype(BF16), w_ref[...])
    w = z.shape[1] // 2
    o_ref[...] = (z[:, :w] * jax.nn.sigmoid(z[:, w:])).astype(BF16)


def s5_post(y_nat, pu, d, w_glu_bf, n_rows):
    nt = n_rows
    return pl.pallas_call(
        _s5_post_kernel,
        grid=(nt // TM,),
        in_specs=[
            pl.BlockSpec((TM, 256), lambda j: (j, 0)),
            pl.BlockSpec((TM, 256), lambda j: (j, 0)),
            pl.BlockSpec((1, 256), lambda j: (0, 0)),
            pl.BlockSpec((256, 512), lambda j: (0, 0)),
        ],
        out_specs=pl.BlockSpec((TM, 256), lambda j: (j, 0)),
        out_shape=jax.ShapeDtypeStruct((nt, 256), BF16),
        compiler_params=_cparams(("arbitrary",)),
        name="s5_post",
    )(y_nat, pu, d, w_glu_bf)


def na_bias_table(rpb):
    ci = jnp.arange(GRID_W)
    cs = jnp.clip(ci - WIN_COLS // 2, 0, GRID_W - WIN_COLS)
    kc = jnp.arange(GRID_W)
    inside = (kc[None, :] >= cs[:, None]) & (kc[None, :] < cs[:, None] + WIN_COLS)
    col_off = kc[None, :] - ci[:, None] + (WIN_COLS - 1)
    sel = (col_off[:, :, None] == jnp.arange(2 * WIN_COLS - 1)[None, None, :]).astype(F32)
    base = jnp.einsum('hrw,qkw->hrqk', rpb.astype(F32), sel, precision=lax.Precision.HIGHEST)
    base = jnp.where(inside[None, None], base, NEG)
    tab = jnp.stack([base[:, WIN_ROWS - 1 - ty:2 * WIN_ROWS - 1 - ty] for ty in range(WIN_ROWS)], axis=0)
    tab = tab.transpose(0, 1, 3, 2, 4)
    return tab.reshape(WIN_ROWS, NA_HEADS * GRID_W, WIN_ROWS * GRID_W)


def _lane_blocks(x, op):
    out = x[:, 0:LANES]
    for i in range(1, x.shape[1] // LANES):
        out = op(out, x[:, i * LANES:(i + 1) * LANES])
    return out


def _heads_stack(q):
    lane = lax.broadcasted_iota(I32, q.shape, 1)
    zero = jnp.zeros_like(q)
    return jnp.concatenate([jnp.where(lane // HEAD_DIM == hh, q, zero) for hh in range(NA_HEADS)], axis=0)


def _heads_merge(o4, m):
    lane = lax.broadcasted_iota(I32, (m, o4.shape[1]), 1)
    out = jnp.zeros((m, o4.shape[1]), F32)
    for hh in range(NA_HEADS):
        out = jnp.where(lane // HEAD_DIM == hh, o4[hh * m:(hh + 1) * m], out)
    return out


def _na_kernel(rows_per_step, n_rows, lat_steps, q_ref, k_ref, v_ref, kc_ref, vc_ref, tab_ref, o_ref):
    i = pl.program_id(1)
    kc = kc_ref[...]
    vc = vc_ref[...]
    nkeys = WIN_ROWS * GRID_W

    @pl.when(i < lat_steps)
    def _():
        def body(a, carry):
            r = i * rows_per_step + a
            rs = jnp.clip(r - WIN_ROWS // 2, 0, n_rows - WIN_ROWS)
            ty = r - rs
            q0 = pl.multiple_of(a * GRID_W, GRID_W)
            k0 = pl.multiple_of(rs * GRID_W, GRID_W)
            q4 = _heads_stack(q_ref[pl.ds(q0, GRID_W), :])
            kw = k_ref[pl.ds(k0, nkeys), :]
            vw = v_ref[pl.ds(k0, nkeys), :]
            s = _dot_nt(q4, kw) + tab_ref[ty]
            sc = _dot_nt(q4, kc)
            m = jnp.max(jnp.maximum(_lane_blocks(s, jnp.maximum), _lane_blocks(sc, jnp.maximum)),
                        axis=-1, keepdims=True)
            e = jnp.exp(s - m)
            ec = jnp.exp(sc - m)
            den = jnp.sum(_lane_blocks(e, jnp.add) + _lane_blocks(ec, jnp.add), axis=-1, keepdims=True)
            o4 = (_dot(e.astype(BF16), vw) + _dot(ec.astype(BF16), vc)) / den
            o_ref[pl.ds(q0, GRID_W), :] = _heads_merge(o4, GRID_W).astype(BF16)
            return carry

        lax.fori_loop(0, rows_per_step, body, 0, unroll=True)

    @pl.when(i >= lat_steps)
    def _():
        q4 = _heads_stack(q_ref[...])
        s = _dot_nt(q4, kc)
        m = jnp.max(_lane_blocks(s, jnp.maximum), axis=-1, keepdims=True)
        e = jnp.exp(s - m)
        den = jnp.sum(_lane_blocks(e, jnp.add), axis=-1, keepdims=True)
        o4 = _dot(e.astype(BF16), vc) / den
        o_ref[...] = _heads_merge(o4, q_ref.shape[0]).astype(BF16)


def na_attention(q, k, v, tab, layer, nb, seq_len, ctx_len, with_ctx):
    tq = ctx_len
    rows_per_step = tq // GRID_W
    n_rows = seq_len // GRID_W
    lat_steps = seq_len // tq
    ctx_blk0 = nb * seq_len // ctx_len

    def q_blk(b, i):
        return (jnp.where(i < lat_steps, b * lat_steps + i, ctx_blk0 + b), 0)

    n_out = nb * seq_len + (nb * ctx_len if with_ctx else 0)
    return pl.pallas_call(
        functools.partial(_na_kernel, rows_per_step, n_rows, lat_steps),
        grid=(nb, lat_steps + (1 if with_ctx else 0)),
        in_specs=[
            pl.BlockSpec((tq, 256), q_blk),
            pl.BlockSpec((seq_len, 256), lambda b, i: (b, 0)),
            pl.BlockSpec((seq_len, 256), lambda b, i: (b, 0)),
            pl.BlockSpec((ctx_len, 256), lambda b, i: (ctx_blk0 + b, 0)),
            pl.BlockSpec((ctx_len, 256), lambda b, i: (ctx_blk0 + b, 0)),
            pl.BlockSpec((None,) + tab.shape[1:], lambda b, i: (layer, 0, 0, 0)),
        ],
        out_specs=pl.BlockSpec((tq, 256), q_blk),
        out_shape=jax.ShapeDtypeStruct((n_out, 256), BF16),
        compiler_params=_cparams(("arbitrary", "arbitrary")),
        name="na_attention",
    )(q, k, v, k, v, tab)


def _merge_kernel(tiles_per_seq, n_lat_tiles, x_ref, mod_ref, gpre_ref, gpost_ref, gffn_ref,
                  ya_ref, yb_ref, yg_ref, yd_ref, wbg_ref, bbg_ref, wbr_ref, wout_ref,
                  wrh_ref, wrl_ref, br_ref, tri_ref, xn_ref, hf_ref, ti_ref, rk_ref, tw_ref, cnt_ref,
                  tcnt_ref, run_ref):
    x = x_ref[...]
    d = x.shape[1]
    mod = mod_ref[0]
    h = _rms(x, gpre_ref[...]) * (1.0 + mod[1:2]) + mod[0:1]
    hb = h.astype(BF16)
    acc = jnp.zeros_like(x)
    for bi, br_ref_i in enumerate((ya_ref, yb_ref, yg_ref, yd_ref)):
        gate = jax.nn.sigmoid(_dot(hb, wbg_ref[:, bi * d:(bi + 1) * d]) + bbg_ref[:, bi * d:(bi + 1) * d])
        acc = acc + gate * _dot(br_ref_i[...], wbr_ref[bi])
    y = _dot(acc.astype(BF16), wout_ref[...])
    xn = x + mod[2:3] * _rms(y, gpost_ref[...])
    xn_ref[...] = xn
    hf = _rms(xn, gffn_ref[...]) * (1.0 + mod[4:5]) + mod[3:4]
    hf_hi = hf.astype(BF16)
    bits = lax.bitcast_convert_type(hf_hi.astype(F32), jnp.uint32)
    hf_ref[...] = (bits[:, :d // 2] >> 16) | (bits[:, d // 2:] & jnp.uint32(0xFFFF0000))
    hf_lo = (hf - hf_hi.astype(F32)).astype(BF16)
    logits = _dot(hf_hi, wrh_ref[...]) + (_dot(hf_hi, wrl_ref[...]) + _dot(hf_lo, wrh_ref[...])) + br_ref[...]

    @pl.when(pl.program_id(0) == 0)
    def _():
        run_ref[...] = jnp.zeros_like(run_ref)

    ne = run_ref.shape[0]
    tm = logits.shape[0]
    work = jnp.transpose(logits)[:ne]
    erow = lax.broadcasted_iota(I32, work.shape, 0)
    idxs, es, ohs = [], [], []
    v0 = None
    for kk in range(TOP_K):
        m = jnp.max(work, axis=0, keepdims=True)
        idx = jnp.min(jnp.where(work == m, erow, ne), axis=0, keepdims=True)
        oh = erow == idx
        if kk == 0:
            v0 = m
            es.append(jnp.ones_like(m))
        else:
            es.append(jnp.exp(m - v0))
        idxs.append(idx)
        ohs.append(oh)
        work = jnp.where(oh, -jnp.inf, work)
    den = es[0] + es[1] + es[2] + es[3]
    ohf = [o.astype(F32) for o in ohs]
    cnt = _dot(jnp.concatenate([o.astype(BF16) for o in ohf], axis=0), tri_ref[...])
    base = run_ref[:, 0:1]
    r8 = lax.broadcasted_iota(I32, (8, tm), 0)
    r128 = lax.broadcasted_iota(I32, (LANES, tm), 0)
    ti8 = jnp.zeros((8, tm), I32)
    rk8 = jnp.zeros((8, tm), I32)
    twt = jnp.zeros((LANES, tm), F32)
    for kk in range(TOP_K):
        rank = jnp.sum(ohf[kk] * (base + cnt[kk * ne:(kk + 1) * ne]), axis=0, keepdims=True)
        base = base + jnp.sum(ohf[kk], axis=1, keepdims=True)
        ti8 = jnp.where(r8 == kk, idxs[kk], ti8)
        rk8 = jnp.where(r8 == kk, rank.astype(I32), rk8)
        twt = jnp.where(r128 == kk, es[kk] / den, twt)
    tile_cnt = base - run_ref[:, 0:1]
    tcnt_ref[0] = jnp.broadcast_to(tile_cnt, run_ref.shape)
    run_ref[...] = run_ref[...] + jnp.floor((tile_cnt + (SEG - 1.0)) * (1.0 / SEG)) * SEG
    ti_ref[...] = ti8
    rk_ref[...] = rk8
    tw_ref[...] = jnp.transpose(twt)
    cnt_ref[...] = run_ref[...]


def merge_and_route(x, mod, gpre, gpost, gffn, ya, yb, yg, yd, wbg, bbg, wbr, wout, wrh, wrl, br,
                    n_tokens, n_lat, seq_len, n_experts):
    d = x.shape[1]
    n_lat_tiles = n_lat // TM
    tiles_per_seq = seq_len // TM

    def mod_row(j):
        return jnp.where(j < n_lat_tiles, j // tiles_per_seq, 4)

    tri = jnp.triu(jnp.ones((TM, TM), F32), 1).astype(BF16)
    tok = lambda w: pl.BlockSpec((TM, w), lambda j: (j, 0))
    lanes = pl.BlockSpec((8, TM), lambda j: (0, j))
    full = lambda shape: pl.BlockSpec(shape, lambda j: tuple(0 for _ in shape))
    return pl.pallas_call(
        functools.partial(_merge_kernel, tiles_per_seq, n_lat_tiles),
        grid=(n_tokens // TM,),
        in_specs=[
            tok(d),
            pl.BlockSpec((1, 6, d), lambda j: (mod_row(j), 0, 0)),
            full((1, d)), full((1, d)), full((1, d)),
            tok(256), tok(256), tok(256), tok(256),
            full(wbg.shape), full(bbg.shape), full(wbr.shape), full(wout.shape),
            full(wrh.shape), full(wrl.shape), full(br.shape), full(tri.shape),
        ],
        out_specs=[tok(d), tok(d // 2), lanes, lanes, tok(LANES), full((n_experts, LANES)),
                   pl.BlockSpec((1, n_experts, LANES), lambda j: (j, 0, 0))],
        out_shape=[
            jax.ShapeDtypeStruct((n_tokens, d), F32),
            jax.ShapeDtypeStruct((n_tokens, d // 2), jnp.uint32),
            jax.ShapeDtypeStruct((8, n_tokens), I32),
            jax.ShapeDtypeStruct((8, n_tokens), I32),
            jax.ShapeDtypeStruct((n_tokens, LANES), F32),
            jax.ShapeDtypeStruct((n_experts, LANES), F32),
            jax.ShapeDtypeStruct((n_tokens // TM, n_experts, LANES), F32),
        ],
        scratch_shapes=[pltpu.VMEM((n_experts, LANES), F32)],
        compiler_params=_cparams(("arbitrary",)),
        name="merge_and_route",
    )(x, mod, gpre, gpost, gffn, ya, yb, yg, yd, wbg, bbg, wbr, wout, wrh, wrl, br, tri)


def _expert_kernel(be_ref, nv_ref, x_ref, w1_ref, b1_ref, w2_ref, b2_ref, o_ref, w1b_ref, w2b_ref):
    i = pl.program_id(0)
    prev = be_ref[jnp.maximum(i - 1, 0)]
    changed = jnp.logical_or(i == 0, be_ref[i] != prev)

    @pl.when(jnp.logical_and(i < nv_ref[0], changed))
    def _():
        w1b_ref[...] = w1_ref[0].astype(BF16)
        w2b_ref[...] = w2_ref[0].astype(BF16)

    @pl.when(i < nv_ref[0])
    def _():
        w = x_ref[...]
        half = w.shape[1]
        x_lo = lax.bitcast_convert_type(w << 16, F32).astype(BF16)
        x_hi = lax.bitcast_convert_type(w & jnp.uint32(0xFFFF0000), F32).astype(BF16)
        h = _dot(x_lo, w1b_ref[0:half, :]) + _dot(x_hi, w1b_ref[half:, :]) + b1_ref[0]
        de = h.shape[1] // 2
        g = jnp.minimum(h[:, :de], SWIGLU_LIMIT)
        lin = jnp.clip(h[:, de:], -SWIGLU_LIMIT, SWIGLU_LIMIT)
        act = g * jax.nn.sigmoid(SWIGLU_ALPHA * g) * (lin + 1.0)
        y = _dot(act.astype(BF16), w2b_ref[...]) + b2_ref[0]
        bits = lax.bitcast_convert_type(y.astype(BF16).astype(F32), jnp.uint32)
        o_ref[...] = (bits[:, :half] >> 16) | (bits[:, half:] & jnp.uint32(0xFFFF0000))

    @pl.when(i >= nv_ref[0])
    def _():
        o_ref[...] = jnp.zeros_like(o_ref)


def moe_experts(xb, block_e, n_valid, w1, b1, w2, b2, layer):
    p = xb.shape[0]
    depth, ne, d, de2 = w1.shape
    n_blocks = p // MOE_BM
    grid_spec = pltpu.PrefetchScalarGridSpec(
        num_scalar_prefetch=2,
        grid=(n_blocks,),
        in_specs=[
            pl.BlockSpec((MOE_BM, d // 2), lambda i, be, nv: (i, 0)),
            pl.BlockSpec((None, 1, d, de2), lambda i, be, nv: (layer, be[i], 0, 0)),
            pl.BlockSpec((None, 1, 1, de2), lambda i, be, nv: (layer, be[i], 0, 0)),
            pl.BlockSpec((None, 1, de2 // 2, d), lambda i, be, nv: (layer, be[i], 0, 0)),
            pl.BlockSpec((None, 1, 1, d), lambda i, be, nv: (layer, be[i], 0, 0)),
        ],
        out_specs=pl.BlockSpec((MOE_BM, d // 2), lambda i, be, nv: (i, 0)),
        scratch_shapes=[pltpu.VMEM((d, de2), BF16), pltpu.VMEM((de2 // 2, d), BF16)],
    )
    return pl.pallas_call(
        _expert_kernel,
        grid_spec=grid_spec,
        out_shape=jax.ShapeDtypeStruct((p, d // 2), jnp.uint32),
        compiler_params=_cparams(("arbitrary",)),
        name="moe_experts",
    )(block_e, n_valid, xb, w1, b1.reshape(depth, ne, 1, de2), w2, b2.reshape(depth, ne, 1, d))


def _combine_kernel(n_experts, src_ref, len_ref, off_ref, dlt_ref, x_ref, mod_ref, g_ref, tw_ref, ti_ref,
                    rk_ref, yblk_ref, o_ref, gbuf, sem):
    j = pl.program_id(0)
    n = pl.num_programs(0)
    slot = j % 2
    tm = x_ref.shape[0]
    rows = gbuf.shape[1]

    def seg_copy(tile, s, e, c):
        base = tile * n_experts + e
        src = pl.multiple_of(src_ref[base] + c * SEG, SEG)
        dst = pl.multiple_of(off_ref[base] + c * SEG, SEG)
        return pltpu.make_async_copy(yblk_ref.at[pl.ds(src, SEG)], gbuf.at[s, pl.ds(dst, SEG)], sem.at[s])

    def for_each_piece(tile, s, fn):
        for e in range(n_experts):
            pieces = (len_ref[tile * n_experts + e] + (SEG - 1)) // SEG

            def body(c, carry, e=e):
                fn(seg_copy(tile, s, e, c))
                return carry
            lax.fori_loop(0, pieces, body, 0)

    @pl.when(j == 0)
    def _():
        gbuf[...] = jnp.zeros_like(gbuf)
        for_each_piece(j, slot, lambda cp: cp.start())

    @pl.when(j + 1 < n)
    def _():
        for_each_piece(j + 1, 1 - slot, lambda cp: cp.start())

    for_each_piece(j, slot, lambda cp: cp.wait())

    ti = ti_ref[...]
    lp = rk_ref[...]
    for e in range(n_experts):
        lp = lp + jnp.where(ti == e, dlt_ref[j * n_experts + e], 0)
    r128 = lax.broadcasted_iota(I32, (LANES, tm), 0)
    lp128 = jnp.zeros((LANES, tm), F32)
    for kk in range(TOP_K):
        lp128 = jnp.where(r128 == kk, lp[kk:kk + 1].astype(F32), lp128)
    lpn = jnp.transpose(lp128)
    tw = tw_ref[...]
    y_lo = jnp.zeros((tm, gbuf.shape[2]), F32)
    y_hi = jnp.zeros((tm, gbuf.shape[2]), F32)
    for c0 in range(0, rows, COMBINE_KC):
        col = (lax.broadcasted_iota(I32, (tm, COMBINE_KC), 1) + c0).astype(F32)
        q = jnp.zeros((tm, COMBINE_KC), F32)
        for kk in range(TOP_K):
            q = q + jnp.where(col == lpn[:, kk:kk + 1], tw[:, kk:kk + 1], 0.0)
        q = q.astype(BF16)
        w = gbuf[slot, c0:c0 + COMBINE_KC, :]
        y_lo = y_lo + _dot(q, lax.bitcast_convert_type(w << 16, F32).astype(BF16))
        y_hi = y_hi + _dot(q, lax.bitcast_convert_type(w & jnp.uint32(0xFFFF0000), F32).astype(BF16))
    y = jnp.concatenate([y_lo, y_hi], axis=1)
    o_ref[...] = x_ref[...] + mod_ref[0, 5:6, :] * _rms(y, g_ref[...])


def moe_combine(seg_src, seg_len, seg_off, seg_dlt, xn, mod, gain, tw, ti_t, rk_t, yblk,
                n_tokens, n_lat, seq_len, n_experts):
    d = xn.shape[1]
    n_lat_tiles = n_lat // TM
    tiles_per_seq = seq_len // TM
    n_tiles = n_tokens // TM
    rows = TM * TOP_K + n_experts * SEG

    def mod_row(j):
        return jnp.where(j < n_lat_tiles, j // tiles_per_seq, 4)

    grid_spec = pltpu.PrefetchScalarGridSpec(
        num_scalar_prefetch=4,
        grid=(n_tiles,),
        in_specs=[
            pl.BlockSpec((TM, d), lambda j, *_: (j, 0)),
            pl.BlockSpec((1, 6, d), lambda j, *_: (mod_row(j), 0, 0)),
            pl.BlockSpec((1, d), lambda j, *_: (0, 0)),
            pl.BlockSpec((TM, LANES), lambda j, *_: (j, 0)),
            pl.BlockSpec((8, TM), lambda j, *_: (0, j)),
            pl.BlockSpec((8, TM), lambda j, *_: (0, j)),
            pl.BlockSpec(memory_space=pl.ANY),
        ],
        out_specs=pl.BlockSpec((TM, d), lambda j, *_: (j, 0)),
        scratch_shapes=[pltpu.VMEM((2, rows, d // 2), jnp.uint32), pltpu.SemaphoreType.DMA((2,))],
    )
    return pl.pallas_call(
        functools.partial(_combine_kernel, n_experts),
        grid_spec=grid_spec,
        out_shape=jax.ShapeDtypeStruct((n_tokens, d), F32),
        compiler_params=_cparams(("arbitrary",)),
        name="moe_combine",
    )(seg_src, seg_len, seg_off, seg_dlt, xn, mod, gain, tw, ti_t, rk_t, yblk)


def _dest_kernel(n_experts, ps_ref, ti_ref, rk_ref, o_ref):
    ti = ti_ref[...]
    dest = rk_ref[...]
    for e in range(n_experts):
        dest = dest + jnp.where(ti == e, ps_ref[e], 0)
    o_ref[0] = dest


def moe_dest(pstart, ti_t, rk_t):
    n_tokens = ti_t.shape[1]
    lanes = pl.BlockSpec((8, TM), lambda j, ps: (0, j))
    dest = pl.pallas_call(
        functools.partial(_dest_kernel, pstart.shape[0]),
        grid_spec=pltpu.PrefetchScalarGridSpec(
            num_scalar_prefetch=1, grid=(n_tokens // TM,), in_specs=[lanes, lanes],
            out_specs=pl.BlockSpec((1, 8, TM), lambda j, ps: (j, 0, 0))),
        out_shape=jax.ShapeDtypeStruct((n_tokens // TM, 8, TM), I32),
        compiler_params=_cparams(("arbitrary",)),
        name="moe_dest",
    )(pstart, ti_t, rk_t)
    return dest.reshape(-1)


def _dispatch_kernel(dest_ref, hf_ref, xb_in_ref, xb_ref, sem):
    del xb_in_ref
    tm = hf_ref.shape[0]

    def row_copy(t, kk):
        return pltpu.make_async_copy(hf_ref.at[pl.ds(t, 1)], xb_ref.at[pl.ds(dest_ref[kk * tm + t], 1)], sem)

    def start(i, carry):
        t0 = pl.multiple_of(i * DMA_UNROLL, DMA_UNROLL)
        for u in range(DMA_UNROLL):
            for kk in range(TOP_K):
                row_copy(t0 + u, kk).start()
        return carry
    lax.fori_loop(0, tm // DMA_UNROLL, start, 0)

    def wait(i, carry):
        t0 = pl.multiple_of(i * DMA_UNROLL, DMA_UNROLL)
        for u in range(DMA_UNROLL):
            for kk in range(TOP_K):
                row_copy(t0 + u, kk).wait()
        return carry
    lax.fori_loop(0, tm // DMA_UNROLL, wait, 0)


def moe_dispatch(dest_t, hfp, n_rows):
    n_tokens, w = hfp.shape
    xb0 = jnp.zeros((n_rows, w), hfp.dtype)
    return pl.pallas_call(
        _dispatch_kernel,
        grid=(n_tokens // TM,),
        in_specs=[
            pl.BlockSpec((8 * TM,), lambda j: (j,), memory_space=pltpu.SMEM),
            pl.BlockSpec((TM, w), lambda j: (j, 0)),
            pl.BlockSpec(memory_space=pl.ANY),
        ],
        out_specs=pl.BlockSpec(memory_space=pl.ANY),
        out_shape=jax.ShapeDtypeStruct((n_rows, w), hfp.dtype),
        input_output_aliases={2: 0},
        scratch_shapes=[pltpu.SemaphoreType.DMA(())],
        compiler_params=_cparams(("arbitrary",)),
        name="moe_dispatch",
    )(dest_t, hfp, xb0)


def kernel(x, c, ctx, c_ctx, ada_w, ada_b, norm_mix_pre, norm_mix_post, norm_ffn_pre, norm_ffn_post,
           w_in, w_branch_gate, b_branch_gate, w_branch, w_out, pool_w, pool_scale,
           ssm_a_re, ssm_a_im, ssm_log_dt, ssm_b_re, ssm_b_im, ssm_c_re, ssm_c_im, ssm_d, ssm_w_glu,
           gmlp_ln_g, gmlp_ln_b, gmlp_w_s, gmlp_b_s, na_rpb,
           router_w, router_b, expert_w1, expert_b1, expert_w2, expert_b2):
    nb, seq_len, d = x.shape
    ctx_len = ctx.shape[1]
    depth = ada_w.shape[0]
    n_lat = nb * seq_len
    n_all = n_lat + nb * ctx_len
    n_experts = router_w.shape[-1]
    assert nb == 4 and seq_len % TM == 0 and (nb * ctx_len) % TM == 0 and ctx_len % TP == 0

    xs = jnp.concatenate([x.reshape(n_lat, d), ctx.reshape(nb * ctx_len, d)], axis=0)
    cvec = jnp.concatenate([c, c_ctx[None, :], jnp.zeros((3, d), F32)], axis=0)
    mods = ada_modulation(cvec, ada_w, ada_b).reshape(depth, 8, 6, d)
    cos_t, sin_t = rope_tables(seq_len)
    kflat, sall, wst, dec = jax.vmap(s5_tables)(ssm_a_re, ssm_a_im, ssm_log_dt, ssm_b_re, ssm_b_im,
                                                ssm_c_re, ssm_c_im)
    na_tab = jax.vmap(na_bias_table)(na_rpb)

    out = None
    for l in range(depth):
        last = l == depth - 1
        mod = mods[l]
        row = lambda v: v.reshape(1, -1)
        pa, pu, pg, q, k, v = premix(xs, mod, row(norm_mix_pre[l]), w_in[l].astype(BF16), cos_t, sin_t,
                                     n_lat, seq_len)

        n_mix = n_lat if last else n_all
        w_bd = jax.scipy.linalg.block_diag(*[pool_w[l, g] for g in range(POOL_GROUPS)]).astype(BF16)
        ya = pool_mix(pa, w_bd, row(pool_scale[l]), n_mix, n_lat, seq_len, ctx_len)

        bs_full = jnp.repeat(gmlp_b_s[l].T, 256 // GMLP_HEADS, axis=1)
        yg = gmlp_mix(pg, row(gmlp_ln_g[l]), row(gmlp_ln_b[l]), gmlp_w_s[l].astype(BF16), bs_full, n_mix)

        y_g = s5_core(s5_to_groups(pu), kflat, sall, wst, dec, l, ctx_len // SSM_T, seq_len // SSM_T, nb)
        yb = s5_post(s5_from_groups(y_g), pu, row(ssm_d[l]), ssm_w_glu[l].astype(BF16), n_mix)

        yd = na_attention(q, k, v, na_tab, l, nb, seq_len, ctx_len, not last)

        rw = jnp.pad(router_w[l], ((0, 0), (0, LANES - n_experts)))
        rw_hi = rw.astype(BF16)
        rw_lo = (rw - rw_hi.astype(F32)).astype(BF16)
        rb = jnp.pad(router_b[l], (0, LANES - n_experts), constant_values=NEG).reshape(1, LANES)
        n_tok = n_lat if last else n_all
        xn, hfp, ti_t, rk_t, tw, counts, tcnt = merge_and_route(
            xs, mod, row(norm_mix_pre[l]), row(norm_mix_post[l]), row(norm_ffn_pre[l]),
            ya, yb, yg, yd, w_branch_gate[l].astype(BF16), row(b_branch_gate[l]),
            w_branch[l].astype(BF16), w_out[l].astype(BF16), rw_hi, rw_lo, rb, n_tok, n_lat, seq_len, n_experts)

        cnt = counts[:, 0].astype(I32)
        padded = (cnt + MOE_BM - 1) // MOE_BM * MOE_BM
        pend = jnp.cumsum(padded)
        n_blocks = (n_tok * TOP_K + (n_tok // TM) * n_experts * (SEG - 1)) // MOE_BM + 1 + n_experts
        blk_start = jnp.arange(n_blocks, dtype=I32) * MOE_BM
        block_e = jnp.minimum(jnp.sum((pend[None, :] <= blk_start[:, None]).astype(I32), axis=1), n_experts - 1)
        n_valid = (pend[-1] // MOE_BM).astype(I32).reshape(1)

        dest_t = moe_dest(pend - padded, ti_t, rk_t)
        xb = moe_dispatch(dest_t, hfp, n_blocks * MOE_BM)
        yblk = moe_experts(xb, block_e, n_valid, expert_w1, expert_b1, expert_w2, expert_b2, l)
        tc = tcnt[:, :, 0].astype(I32)
        c8 = (tc + SEG - 1) // SEG * SEG
        run = jnp.cumsum(c8, axis=0) - c8
        seg_off = jnp.cumsum(c8, axis=1) - c8
        seg_src = (pend - padded)[None, :] + run
        xs = moe_combine(seg_src.reshape(-1), tc.reshape(-1), seg_off.reshape(-1), (seg_off - run).reshape(-1),
                         xn, mod, row(norm_ffn_post[l]), tw, ti_t, rk_t, yblk, n_tok, n_lat, seq_len, n_experts)
        out = xs
    return out.reshape(nb, seq_len, d)
```

```python
import functools
import math

import jax
import jax.numpy as jnp
import jax.scipy.linalg
from jax import lax
from jax.experimental import pallas as pl
from jax.experimental.pallas import tpu as pltpu

F32 = jnp.float32
BF16 = jnp.bfloat16
I32 = jnp.int32

GRID_W = 64
EPS = 1e-6
POOL_GROUPS = 4
POOL_WINDOWS = (2, 4, 8, 16)
SSM_CH = 16
SSM_N = 64
GMLP_HEADS = 4
CHUNK = 128
NA_HEADS = 4
HEAD_DIM = 64
WIN_ROWS = 8
WIN_COLS = 16
ROPE_BASE = 10000.0
BRANCH_W = 256
N_BRANCH = 4
TOP_K = 4
SWIGLU_LIMIT = 7.0
SWIGLU_ALPHA = 1.702

TM = 512
TP = 256
POOL_HALO = 16
SSM_T = 32
MOE_BM = 256
DMA_UNROLL = 8
SEG = 8
COMBINE_KC = 256
LANES = 128
NEG = -1e30
VMEM_LIMIT = 56 * 1024 * 1024


def _cparams(sem):
    return pltpu.CompilerParams(dimension_semantics=sem, vmem_limit_bytes=VMEM_LIMIT)


def _dot(a, b):
    return jnp.dot(a, b, preferred_element_type=F32)


def _dot_nt(a, b):
    return lax.dot_general(a, b, (((1,), (1,)), ((), ())), preferred_element_type=F32)


def _ada_kernel(c_ref, w_ref, b_ref, o_ref):
    c = c_ref[...]
    s = c * jax.nn.sigmoid(c)
    o_ref[0] = jnp.dot(s, w_ref[0], preferred_element_type=F32,
                       precision=lax.Precision.HIGHEST) + b_ref[0]


def ada_modulation(cvec, ada_w, ada_b):
    depth, d, n = ada_w.shape
    tn = 1536
    return pl.pallas_call(
        _ada_kernel,
        grid=(depth, n // tn),
        in_specs=[
            pl.BlockSpec((8, d), lambda l, j: (0, 0)),
            pl.BlockSpec((1, d, tn), lambda l, j: (l, 0, j)),
            pl.BlockSpec((1, 1, tn), lambda l, j: (l, 0, j)),
        ],
        out_specs=pl.BlockSpec((1, 8, tn), lambda l, j: (l, 0, j)),
        out_shape=jax.ShapeDtypeStruct((depth, 8, n), F32),
        compiler_params=_cparams(("arbitrary", "arbitrary")),
        name="ada_modulation",
    )(cvec, ada_w, ada_b.reshape(depth, 1, n))


def _rms(x, g):
    return x * lax.rsqrt(jnp.mean(x * x, axis=-1, keepdims=True) + EPS) * g


def _premix_kernel(n_lat_tiles, x_ref, mod_ref, g_ref, w_ref, cos_ref, sin_ref,
                   pa_ref, pu_ref, pg_ref, q_ref, k_ref, v_ref):
    j = pl.program_id(0)
    x = x_ref[...]
    h = _rms(x, g_ref[...])
    h = h * (1.0 + mod_ref[0, 1:2, :]) + mod_ref[0, 0:1, :]
    p = _dot(h.astype(BF16), w_ref[...])
    wa = p.shape[1] - 768
    pa_ref[...] = p[:, 0:256]
    pu_ref[...] = p[:, 256:512]
    pg_ref[...] = p[:, 512:wa]
    q = p[:, wa:wa + 256]
    k = p[:, wa + 256:wa + 512]
    v = p[:, wa + 512:wa + 768]
    is_lat = j < n_lat_tiles
    cos = jnp.where(is_lat, cos_ref[...], 1.0)
    sin = jnp.where(is_lat, sin_ref[...], 0.0)
    lane = lax.broadcasted_iota(I32, q.shape, 1)
    first = (lane % 32) < 16

    def rope(t):
        partner = jnp.where(first, pltpu.roll(t, 256 - 16, axis=1), pltpu.roll(t, 16, axis=1))
        return t * cos + partner * sin

    q_ref[...] = (rope(q) * (HEAD_DIM ** -0.5)).astype(BF16)
    k_ref[...] = rope(k).astype(BF16)
    v_ref[...] = v.astype(BF16)


def premix(x, mod, gain, w_in_bf, cos_t, sin_t, n_lat, seq_len):
    nt, d = x.shape
    in_w = w_in_bf.shape[1]
    wa = in_w - 768
    n_lat_tiles = n_lat // TM
    tiles_per_seq = seq_len // TM

    def mod_row(j):
        return jnp.where(j < n_lat_tiles, j // tiles_per_seq, 4)

    def tab_blk(j):
        return jnp.where(j < n_lat_tiles, j % tiles_per_seq, 0)

    return pl.pallas_call(
        functools.partial(_premix_kernel, n_lat_tiles),
        grid=(nt // TM,),
        in_specs=[
            pl.BlockSpec((TM, d), lambda j: (j, 0)),
            pl.BlockSpec((1, 6, d), lambda j: (mod_row(j), 0, 0)),
            pl.BlockSpec((1, d), lambda j: (0, 0)),
            pl.BlockSpec((d, in_w), lambda j: (0, 0)),
            pl.BlockSpec((TM, 256), lambda j: (tab_blk(j), 0)),
            pl.BlockSpec((TM, 256), lambda j: (tab_blk(j), 0)),
        ],
        out_specs=[
            pl.BlockSpec((TM, 256), lambda j: (j, 0)),
            pl.BlockSpec((TM, 256), lambda j: (j, 0)),
            pl.BlockSpec((TM, wa - 512), lambda j: (j, 0)),
            pl.BlockSpec((TM, 256), lambda j: (j, 0)),
            pl.BlockSpec((TM, 256), lambda j: (j, 0)),
            pl.BlockSpec((TM, 256), lambda j: (j, 0)),
        ],
        out_shape=[
            jax.ShapeDtypeStruct((nt, 256), F32),
            jax.ShapeDtypeStruct((nt, 256), F32),
            jax.ShapeDtypeStruct((nt, wa - 512), F32),
            jax.ShapeDtypeStruct((nt, 256), BF16),
            jax.ShapeDtypeStruct((nt, 256), BF16),
            jax.ShapeDtypeStruct((nt, 256), BF16),
        ],
        compiler_params=_cparams(("arbitrary",)),
        name="premix",
    )(x, mod, gain, w_in_bf, cos_t, sin_t)


def rope_tables(seq_len):
    t = jnp.arange(seq_len)
    half = HEAD_DIM // 2
    nf = half // 2
    inv = ROPE_BASE ** (-jnp.arange(nf, dtype=F32) / nf)
    d = jnp.arange(HEAD_DIM)
    pos = jnp.where((d // half)[None, :] == 0, (t // GRID_W)[:, None], (t % GRID_W)[:, None]).astype(F32)
    ang = pos * inv[(d % half) % nf][None, :]
    sign = jnp.where((d % half) < nf, -1.0, 1.0)[None, :]
    cos = jnp.tile(jnp.cos(ang), (1, NA_HEADS))
    sin = jnp.tile(jnp.sin(ang) * sign, (1, NA_HEADS))
    return cos.astype(F32), sin.astype(F32)


def _pool_kernel(n_lat_tiles, tiles_per_seq, ctx_tiles_per_seq, cur_ref, prev_ref, next_ref,
                 w_ref, sc_ref, o_ref, ext_ref):
    j = pl.program_id(0)
    is_lat = j < n_lat_tiles
    t0 = jnp.where(is_lat, (j % tiles_per_seq) * TP, ((j - n_lat_tiles) % ctx_tiles_per_seq) * TP)
    slen = jnp.where(is_lat, tiles_per_seq * TP, ctx_tiles_per_seq * TP)
    u = cur_ref[...]
    h = POOL_HALO
    ext_ref[0:h, :] = jnp.where(t0 > 0, prev_ref[...], 0.0)
    ext_ref[h:h + TP, :] = u
    ext_ref[h + TP:h + TP + h, :] = jnp.where(t0 + TP < slen, next_ref[...], 0.0)
    t = t0 + lax.broadcasted_iota(I32, (TP, 1), 0)
    lane = lax.broadcasted_iota(I32, (TP, u.shape[1]), 1)
    grp = lane // (u.shape[1] // POOL_GROUPS)
    mean = jnp.zeros_like(u)
    for gi, w in enumerate(POOL_WINDOWS):
        acc = ext_ref[pl.ds(h - w // 2, TP), :]
        for o in range(1, w):
            acc = acc + ext_ref[pl.ds(h - w // 2 + o, TP), :]
        cnt = (jnp.minimum(t - w // 2 + w, slen) - jnp.maximum(t - w // 2, 0)).astype(F32)
        mean = jnp.where(grp == gi, acc / cnt, mean)
    y = _dot((mean - u).astype(BF16), w_ref[...]) * sc_ref[...]
    o_ref[...] = y.astype(BF16)


def pool_mix(p1, w_bd_bf, scale, n_rows, n_lat, seq_len, ctx_len):
    nt = n_rows
    hb = TP // POOL_HALO
    n_halo_blocks = p1.shape[0] // POOL_HALO
    return pl.pallas_call(
        functools.partial(_pool_kernel, n_lat // TP, seq_len // TP, ctx_len // TP),
        grid=(nt // TP,),
        in_specs=[
            pl.BlockSpec((TP, 256), lambda j: (j, 0)),
            pl.BlockSpec((POOL_HALO, 256), lambda j: (jnp.maximum(j * hb - 1, 0), 0)),
            pl.BlockSpec((POOL_HALO, 256), lambda j: (jnp.minimum((j + 1) * hb, n_halo_blocks - 1), 0)),
            pl.BlockSpec((256, 256), lambda j: (0, 0)),
            pl.BlockSpec((1, 256), lambda j: (0, 0)),
        ],
        out_specs=pl.BlockSpec((TP, 256), lambda j: (j, 0)),
        out_shape=jax.ShapeDtypeStruct((n_rows, 256), BF16),
        scratch_shapes=[pltpu.VMEM((TP + 2 * POOL_HALO, 256), F32)],
        compiler_params=_cparams(("arbitrary",)),
        name="pool_mix",
    )(p1, p1, p1, w_bd_bf, scale)


def _gmlp_kernel(p_ref, g_ref, b_ref, ws_ref, bs_ref, o_ref):
    uv = jax.nn.gelu(p_ref[...])
    w = uv.shape[1] // 2
    u = uv[:, :w]
    v = uv[:, w:]
    mu = jnp.mean(v, axis=-1, keepdims=True)
    var = jnp.mean((v - mu) ** 2, axis=-1, keepdims=True)
    v = ((v - mu) * lax.rsqrt(var + EPS) * g_ref[...] + b_ref[...]).astype(BF16)
    lane = lax.broadcasted_iota(I32, (CHUNK, w), 1)
    head = lane // (w // GMLP_HEADS)
    for c in range(TP // CHUNK):
        vc = v[c * CHUNK:(c + 1) * CHUNK]
        sv = bs_ref[...]
        for hh in range(GMLP_HEADS):
            sv = sv + jnp.where(head == hh, _dot(ws_ref[hh], vc), 0.0)
        o_ref[c * CHUNK:(c + 1) * CHUNK, :] = (u[c * CHUNK:(c + 1) * CHUNK] * sv).astype(BF16)


def gmlp_mix(p1, ln_g, ln_b, ws_bf, bs_full, n_rows):
    return pl.pallas_call(
        _gmlp_kernel,
        grid=(n_rows // TP,),
        in_specs=[
            pl.BlockSpec((TP, 512), lambda j: (j, 0)),
            pl.BlockSpec((1, 256), lambda j: (0, 0)),
            pl.BlockSpec((1, 256), lambda j: (0, 0)),
            pl.BlockSpec((GMLP_HEADS, CHUNK, CHUNK), lambda j: (0, 0, 0)),
            pl.BlockSpec((CHUNK, 256), lambda j: (0, 0)),
        ],
        out_specs=pl.BlockSpec((TP, 256), lambda j: (j, 0)),
        out_shape=jax.ShapeDtypeStruct((n_rows, 256), BF16),
        compiler_params=_cparams(("arbitrary",)),
        name="gmlp_mix",
    )(p1, ln_g, ln_b, ws_bf, bs_full)


def s5_tables(a_re, a_im, log_dt, b_re, b_im, c_re, c_im):
    hp = lax.Precision.HIGHEST
    T = SSM_T
    a_re = a_re.astype(F32)
    a_im = a_im.astype(F32)
    dt = jnp.exp(log_dt.astype(F32))[..., None]
    kk = jnp.arange(T + 1, dtype=F32)[:, None, None, None]
    mag = jnp.exp(kk * (a_re * dt)[None])
    pw_re = mag * jnp.cos(kk * (a_im * dt)[None])
    pw_im = mag * jnp.sin(kk * (a_im * dt)[None])

    def cmul(xr, xi, yr, yi):
        return xr * yr - xi * yi, xr * yi + xi * yr

    nr, ni = pw_re[1] - 1.0, pw_im[1]
    den = a_re * a_re + a_im * a_im
    zr, zi = (nr * a_re + ni * a_im) / den, (ni * a_re - nr * a_im) / den
    bb_re, bb_im = cmul(zr[..., None], zi[..., None], b_re.astype(F32), b_im.astype(F32))
    c_re = c_re.astype(F32)
    c_im = c_im.astype(F32)
    G, P = c_re.shape[1], c_re.shape[2]
    e_re, e_im = cmul(c_re[None], c_im[None], pw_re[:T, :, :, None, :], pw_im[:T, :, :, None, :])
    kern = (jnp.einsum('kdgpn,dgnq->dgqpk', e_re, bb_re, precision=hp)
            - jnp.einsum('kdgpn,dgnq->dgqpk', e_im, bb_im, precision=hp))
    kflat = jnp.stack([kern[0], kern[1][..., ::-1]], axis=1).reshape(G, 2, P, P * T)
    s_idx = jnp.arange(T)
    def pack_s(pr, pi, d):
        zr_, zi_ = cmul(pr[..., None], pi[..., None], bb_re[d][None], bb_im[d][None])
        f = lambda z: z.transpose(1, 3, 0, 2).reshape(G, P * T, -1)
        return f(zr_), f(zi_)

    sfr, sfi = pack_s(pw_re[T - 1 - s_idx, 0], pw_im[T - 1 - s_idx, 0], 0)
    sbr, sbi = pack_s(pw_re[s_idx, 1], pw_im[s_idx, 1], 1)
    sall = jnp.concatenate([sfr, sfi, sfi, sfr, sbr, sbi, sbi, sbr], axis=-1)
    def pack_w(pr, pi, d):
        zr_, zi_ = cmul(c_re[d][:, :, :, None], c_im[d][:, :, :, None],
                        pr.transpose(1, 2, 0)[:, None], pi.transpose(1, 2, 0)[:, None])
        f = lambda z: z.transpose(0, 2, 1, 3).reshape(G, -1, P * T)
        return jnp.concatenate([f(zr_), -f(zi_)], axis=1)

    wst = jnp.concatenate([pack_w(pw_re[1 + s_idx, 0], pw_im[1 + s_idx, 0], 0),
                           pack_w(pw_re[T - s_idx, 1], pw_im[T - s_idx, 1], 1)], axis=1)
    rows = []
    for dd in range(2):
        ar, ai = pw_re[T, dd], pw_im[T, dd]
        rows += [jnp.concatenate([ar, ar], -1), jnp.concatenate([-ai, ai], -1), jnp.concatenate([ai, -ai], -1)]
    rows += [jnp.zeros_like(rows[0])] * 2
    dec = jnp.stack(rows, axis=1)
    return kflat.astype(F32), sall.astype(BF16), wst.astype(BF16), dec.astype(F32)


def _s5_kernel(n_ctx_chunks, n_lat_chunks, nb, a_ref, kflat_ref, sall_ref, wst_ref, dec_ref,
               o_ref, x_ref, h_ref, toe_ref):
    T = SSM_T
    kf = kflat_ref[0, 0]
    kb = kflat_ref[0, 1]
    w = kf.shape[1]
    tpos = lax.broadcasted_iota(I32, kf.shape, 1) % T
    for s in range(T):
        row = (jnp.where(tpos >= s, pltpu.roll(kf, s, axis=1), 0.0)
               + jnp.where(tpos <= s, pltpu.roll(kb, (w - (T - 1 - s)) % w, axis=1), 0.0))
        for c in range(w // LANES):
            toe_ref[c, pl.ds(s, SSM_CH, stride=T), :] = row[:, c * LANES:(c + 1) * LANES]
    a = a_ref[0].astype(BF16)
    y = jnp.concatenate([_dot(a, toe_ref[c].astype(BF16)) for c in range(w // LANES)], axis=1)
    x = _dot(a, sall_ref[0])
    for i in range(4):
        x_ref[i] = x[:, i * LANES:(i + 1) * LANES]
    dec = dec_ref[0]
    n_chunks = n_ctx_chunks + n_lat_chunks
    fwd_order = list(range(n_lat_chunks, n_chunks)) + list(range(n_lat_chunks))
    bwd_order = list(range(n_chunks - 1, n_lat_chunks - 1, -1)) + list(range(n_lat_chunks - 1, -1, -1))
    for d, order in enumerate((fwd_order, bwd_order)):
        a1 = dec[3 * d:3 * d + 1]
        a2 = dec[3 * d + 1:3 * d + 2]
        a3 = dec[3 * d + 2:3 * d + 3]
        s1 = jnp.zeros((nb, LANES), F32)
        s2 = jnp.zeros((nb, LANES), F32)
        for c in order:
            if c < n_lat_chunks:
                r = pl.ds(c, nb, stride=n_lat_chunks)
            else:
                r = pl.ds(nb * n_lat_chunks + c - n_lat_chunks, nb, stride=n_ctx_chunks)
            h_ref[d, r, :] = s1
            x1 = x_ref[2 * d, r, :]
            x2 = x_ref[2 * d + 1, r, :]
            s1, s2 = a1 * s1 + a2 * s2 + x1, a1 * s2 + a3 * s1 + x2
    for d in range(2):
        y = y + _dot(h_ref[d].astype(BF16), wst_ref[0, d * LANES:(d + 1) * LANES, :])
    o_ref[0] = y


def s5_core(a_g, kflat, sall, wst, dec, layer, n_ctx_chunks, n_lat_chunks, nb):
    G, rows, w = a_g.shape
    return pl.pallas_call(
        functools.partial(_s5_kernel, n_ctx_chunks, n_lat_chunks, nb),
        grid=(G,),
        in_specs=[
            pl.BlockSpec((1, rows, w), lambda g: (g, 0, 0)),
            pl.BlockSpec((None, 1, 2, SSM_CH, w), lambda g: (layer, g, 0, 0, 0)),
            pl.BlockSpec((None, 1, w, 512), lambda g: (layer, g, 0, 0)),
            pl.BlockSpec((None, 1, 256, w), lambda g: (layer, g, 0, 0)),
            pl.BlockSpec((None, 1, 8, LANES), lambda g: (layer, g, 0, 0)),
        ],
        out_specs=pl.BlockSpec((1, rows, w), lambda g: (g, 0, 0)),
        out_shape=jax.ShapeDtypeStruct((G, rows, w), F32),
        scratch_shapes=[pltpu.VMEM((4, rows, LANES), F32), pltpu.VMEM((2, rows, LANES), F32),
                        pltpu.VMEM((w // LANES, w, LANES), F32)],
        compiler_params=_cparams(("arbitrary",)),
        name="s5_core",
    )(a_g, kflat, sall, wst, dec)


def s5_to_groups(u):
    T = SSM_T
    G = u.shape[1] // SSM_CH
    return u.reshape(-1, T, G, SSM_CH).transpose(2, 0, 3, 1).reshape(G, -1, SSM_CH * T)


def s5_from_groups(y):
    T = SSM_T
    G = y.shape[0]
    return y.reshape(G, -1, SSM_CH, T).transpose(1, 3, 0, 2).reshape(-1, G * SSM_CH)


def _s5_post_kernel(y_ref, u_ref, d_ref, w_ref, o_ref):
    y = jax.nn.gelu(y_ref[...] + d_ref[...] * u_ref[...])
    z = _dot(y.astype(BF16), w_ref[...])
    w = z.shape[1] // 2
    o_ref[...] = (z[:, :w] * jax.nn.sigmoid(z[:, w:])).astype(BF16)


def s5_post(y_nat, pu, d, w_glu_bf, n_rows):
    nt = n_rows
    return pl.pallas_call(
        _s5_post_kernel,
        grid=(nt // TM,),
        in_specs=[
            pl.BlockSpec((TM, 256), lambda j: (j, 0)),
            pl.BlockSpec((TM, 256), lambda j: (j, 0)),
            pl.BlockSpec((1, 256), lambda j: (0, 0)),
            pl.BlockSpec((256, 512), lambda j: (0, 0)),
        ],
        out_specs=pl.BlockSpec((TM, 256), lambda j: (j, 0)),
        out_shape=jax.ShapeDtypeStruct((nt, 256), BF16),
        compiler_params=_cparams(("arbitrary",)),
        name="s5_post",
    )(y_nat, pu, d, w_glu_bf)


def na_bias_table(rpb):
    ci = jnp.arange(GRID_W)
    cs = jnp.clip(ci - WIN_COLS // 2, 0, GRID_W - WIN_COLS)
    kc = jnp.arange(GRID_W)
    inside = (kc[None, :] >= cs[:, None]) & (kc[None, :] < cs[:, None] + WIN_COLS)
    col_off = kc[None, :] - ci[:, None] + (WIN_COLS - 1)
    sel = (col_off[:, :, None] == jnp.arange(2 * WIN_COLS - 1)[None, None, :]).astype(F32)
    base = jnp.einsum('hrw,qkw->hrqk', rpb.astype(F32), sel, precision=lax.Precision.HIGHEST)
    base = jnp.where(inside[None, None], base, NEG)
    tab = jnp.stack([base[:, WIN_ROWS - 1 - ty:2 * WIN_ROWS - 1 - ty] for ty in range(WIN_ROWS)], axis=0)
    tab = tab.transpose(0, 1, 3, 2, 4)
    return tab.reshape(WIN_ROWS, NA_HEADS * GRID_W, WIN_ROWS * GRID_W)


def _lane_blocks(x, op):
    out = x[:, 0:LANES]
    for i in range(1, x.shape[1] // LANES):
        out = op(out, x[:, i * LANES:(i + 1) * LANES])
    return out


def _heads_stack(q):
    lane = lax.broadcasted_iota(I32, q.shape, 1)
    zero = jnp.zeros_like(q)
    return jnp.concatenate([jnp.where(lane // HEAD_DIM == hh, q, zero) for hh in range(NA_HEADS)], axis=0)


def _heads_merge(o4, m):
    lane = lax.broadcasted_iota(I32, (m, o4.shape[1]), 1)
    out = jnp.zeros((m, o4.shape[1]), F32)
    for hh in range(NA_HEADS):
        out = jnp.where(lane // HEAD_DIM == hh, o4[hh * m:(hh + 1) * m], out)
    return out


def _na_kernel(rows_per_step, n_rows, lat_steps, q_ref, k_ref, v_ref, kc_ref, vc_ref, tab_ref, o_ref):
    i = pl.program_id(1)
    kc = kc_ref[...]
    vc = vc_ref[...]
    nkeys = WIN_ROWS * GRID_W

    @pl.when(i < lat_steps)
    def _():
        def body(a, carry):
            r = i * rows_per_step + a
            rs = jnp.clip(r - WIN_ROWS // 2, 0, n_rows - WIN_ROWS)
            ty = r - rs
            q0 = pl.multiple_of(a * GRID_W, GRID_W)
            k0 = pl.multiple_of(rs * GRID_W, GRID_W)
            q4 = _heads_stack(q_ref[pl.ds(q0, GRID_W), :])
            kw = k_ref[pl.ds(k0, nkeys), :]
            vw = v_ref[pl.ds(k0, nkeys), :]
            s = _dot_nt(q4, kw) + tab_ref[ty]
            sc = _dot_nt(q4, kc)
            m = jnp.max(jnp.maximum(_lane_blocks(s, jnp.maximum), _lane_blocks(sc, jnp.maximum)),
                        axis=-1, keepdims=True)
            e = jnp.exp(s - m)
            ec = jnp.exp(sc - m)
            den = jnp.sum(_lane_blocks(e, jnp.add) + _lane_blocks(ec, jnp.add), axis=-1, keepdims=True)
            o4 = (_dot(e.astype(BF16), vw) + _dot(ec.astype(BF16), vc)) / den
            o_ref[pl.ds(q0, GRID_W), :] = _heads_merge(o4, GRID_W).astype(BF16)
            return carry

        lax.fori_loop(0, rows_per_step, body, 0, unroll=True)

    @pl.when(i >= lat_steps)
    def _():
        q4 = _heads_stack(q_ref[...])
        s = _dot_nt(q4, kc)
        m = jnp.max(_lane_blocks(s, jnp.maximum), axis=-1, keepdims=True)
        e = jnp.exp(s - m)
        den = jnp.sum(_lane_blocks(e, jnp.add), axis=-1, keepdims=True)
        o4 = _dot(e.astype(BF16), vc) / den
        o_ref[...] = _heads_merge(o4, q_ref.shape[0]).astype(BF16)


def na_attention(q, k, v, tab, layer, nb, seq_len, ctx_len, with_ctx):
    tq = ctx_len
    rows_per_step = tq // GRID_W
    n_rows = seq_len // GRID_W
    lat_steps = seq_len // tq
    ctx_blk0 = nb * seq_len // ctx_len

    def q_blk(b, i):
        return (jnp.where(i < lat_steps, b * lat_steps + i, ctx_blk0 + b), 0)

    n_out = nb * seq_len + (nb * ctx_len if with_ctx else 0)
    return pl.pallas_call(
        functools.partial(_na_kernel, rows_per_step, n_rows, lat_steps),
        grid=(nb, lat_steps + (1 if with_ctx else 0)),
        in_specs=[
            pl.BlockSpec((tq, 256), q_blk),
            pl.BlockSpec((seq_len, 256), lambda b, i: (b, 0)),
            pl.BlockSpec((seq_len, 256), lambda b, i: (b, 0)),
            pl.BlockSpec((ctx_len, 256), lambda b, i: (ctx_blk0 + b, 0)),
            pl.BlockSpec((ctx_len, 256), lambda b, i: (ctx_blk0 + b, 0)),
            pl.BlockSpec((None,) + tab.shape[1:], lambda b, i: (layer, 0, 0, 0)),
        ],
        out_specs=pl.BlockSpec((tq, 256), q_blk),
        out_shape=jax.ShapeDtypeStruct((n_out, 256), BF16),
        compiler_params=_cparams(("arbitrary", "arbitrary")),
        name="na_attention",
    )(q, k, v, k, v, tab)


def _merge_kernel(tiles_per_seq, n_lat_tiles, x_ref, mod_ref, gpre_ref, gpost_ref, gffn_ref,
                  ya_ref, yb_ref, yg_ref, yd_ref, wbg_ref, bbg_ref, wbr_ref, wout_ref,
                  wrh_ref, wrl_ref, br_ref, tri_ref, xn_ref, hf_ref, ti_ref, rk_ref, tw_ref, cnt_ref,
                  tcnt_ref, run_ref):
    x = x_ref[...]
    d = x.shape[1]
    mod = mod_ref[0]
    h = _rms(x, gpre_ref[...]) * (1.0 + mod[1:2]) + mod[0:1]
    hb = h.astype(BF16)
    acc = jnp.zeros_like(x)
    for bi, br_ref_i in enumerate((ya_ref, yb_ref, yg_ref, yd_ref)):
        gate = jax.nn.sigmoid(_dot(hb, wbg_ref[:, bi * d:(bi + 1) * d]) + bbg_ref[:, bi * d:(bi + 1) * d])
        acc = acc + gate * _dot(br_ref_i[...], wbr_ref[bi])
    y = _dot(acc.astype(BF16), wout_ref[...])
    xn = x + mod[2:3] * _rms(y, gpost_ref[...])
    xn_ref[...] = xn
    hf = _rms(xn, gffn_ref[...]) * (1.0 + mod[4:5]) + mod[3:4]
    hf_hi = hf.astype(BF16)
    bits = lax.bitcast_convert_type(hf_hi.astype(F32), jnp.uint32)
    hf_ref[...] = (bits[:, :d // 2] >> 16) | (bits[:, d // 2:] & jnp.uint32(0xFFFF0000))
    hf_lo = (hf - hf_hi.astype(F32)).astype(BF16)
    logits = _dot(hf_hi, wrh_ref[...]) + (_dot(hf_hi, wrl_ref[...]) + _dot(hf_lo, wrh_ref[...])) + br_ref[...]

    @pl.when(pl.program_id(0) == 0)
    def _():
        run_ref[...] = jnp.zeros_like(run_ref)

    ne = run_ref.shape[0]
    tm = logits.shape[0]
    work = jnp.transpose(logits)[:ne]
    erow = lax.broadcasted_iota(I32, work.shape, 0)
    idxs, es, ohs = [], [], []
    v0 = None
    for kk in range(TOP_K):
        m = jnp.max(work, axis=0, keepdims=True)
        idx = jnp.min(jnp.where(work == m, erow, ne), axis=0, keepdims=True)
        oh = erow == idx
        if kk == 0:
            v0 = m
            es.append(jnp.ones_like(m))
        else:
            es.append(jnp.exp(m - v0))
        idxs.append(idx)
        ohs.append(oh)
        work = jnp.where(oh, -jnp.inf, work)
    den = es[0] + es[1] + es[2] + es[3]
    ohf = [o.astype(F32) for o in ohs]
    cnt = _dot(jnp.concatenate([o.astype(BF16) for o in ohf], axis=0), tri_ref[...])
    base = run_ref[:, 0:1]
    r8 = lax.broadcasted_iota(I32, (8, tm), 0)
    r128 = lax.broadcasted_iota(I32, (LANES, tm), 0)
    ti8 = jnp.zeros((8, tm), I32)
    rk8 = jnp.zeros((8, tm), I32)
    twt = jnp.zeros((LANES, tm), F32)
    for kk in range(TOP_K):
        rank = jnp.sum(ohf[kk] * (base + cnt[kk * ne:(kk + 1) * ne]), axis=0, keepdims=True)
        base = base + jnp.sum(ohf[kk], axis=1, keepdims=True)
        ti8 = jnp.where(r8 == kk, idxs[kk], ti8)
        rk8 = jnp.where(r8 == kk, rank.astype(I32), rk8)
        twt = jnp.where(r128 == kk, es[kk] / den, twt)
    tile_cnt = base - run_ref[:, 0:1]
    tcnt_ref[0] = jnp.broadcast_to(tile_cnt, run_ref.shape)
    run_ref[...] = run_ref[...] + jnp.floor((tile_cnt + (SEG - 1.0)) * (1.0 / SEG)) * SEG
    ti_ref[...] = ti8
    rk_ref[...] = rk8
    tw_ref[...] = jnp.transpose(twt)
    cnt_ref[...] = run_ref[...]


def merge_and_route(x, mod, gpre, gpost, gffn, ya, yb, yg, yd, wbg, bbg, wbr, wout, wrh, wrl, br,
                    n_tokens, n_lat, seq_len, n_experts):
    d = x.shape[1]
    n_lat_tiles = n_lat // TM
    tiles_per_seq = seq_len // TM

    def mod_row(j):
        return jnp.where(j < n_lat_tiles, j // tiles_per_seq, 4)

    tri = jnp.triu(jnp.ones((TM, TM), F32), 1).astype(BF16)
    tok = lambda w: pl.BlockSpec((TM, w), lambda j: (j, 0))
    lanes = pl.BlockSpec((8, TM), lambda j: (0, j))
    full = lambda shape: pl.BlockSpec(shape, lambda j: tuple(0 for _ in shape))
    return pl.pallas_call(
        functools.partial(_merge_kernel, tiles_per_seq, n_lat_tiles),
        grid=(n_tokens // TM,),
        in_specs=[
            tok(d),
            pl.BlockSpec((1, 6, d), lambda j: (mod_row(j), 0, 0)),
            full((1, d)), full((1, d)), full((1, d)),
            tok(256), tok(256), tok(256), tok(256),
            full(wbg.shape), full(bbg.shape), full(wbr.shape), full(wout.shape),
            full(wrh.shape), full(wrl.shape), full(br.shape), full(tri.shape),
        ],
        out_specs=[tok(d), tok(d // 2), lanes, lanes, tok(LANES), full((n_experts, LANES)),
                   pl.BlockSpec((1, n_experts, LANES), lambda j: (j, 0, 0))],
        out_shape=[
            jax.ShapeDtypeStruct((n_tokens, d), F32),
            jax.ShapeDtypeStruct((n_tokens, d // 2), jnp.uint32),
            jax.ShapeDtypeStruct((8, n_tokens), I32),
            jax.ShapeDtypeStruct((8, n_tokens), I32),
            jax.ShapeDtypeStruct((n_tokens, LANES), F32),
            jax.ShapeDtypeStruct((n_experts, LANES), F32),
            jax.ShapeDtypeStruct((n_tokens // TM, n_experts, LANES), F32),
        ],
        scratch_shapes=[pltpu.VMEM((n_experts, LANES), F32)],
        compiler_params=_cparams(("arbitrary",)),
        name="merge_and_route",
    )(x, mod, gpre, gpost, gffn, ya, yb, yg, yd, wbg, bbg, wbr, wout, wrh, wrl, br, tri)


def _expert_kernel(layer, be_ref, nv_ref, nxt_ref, par_ref, x_ref, b1_ref, b2_ref, w1_hbm, w2_hbm, o_ref,
                   w1f_ref, w2f_ref, w1b_ref, w2b_ref, sem):
    i = pl.program_id(0)
    prev = be_ref[jnp.maximum(i - 1, 0)]
    changed = jnp.logical_or(i == 0, be_ref[i] != prev)
    valid = i < nv_ref[0]
    slot = par_ref[i]

    def fetch(e, s):
        return (pltpu.make_async_copy(w1_hbm.at[layer, e], w1f_ref.at[s], sem.at[0, s]),
                pltpu.make_async_copy(w2_hbm.at[layer, e], w2f_ref.at[s], sem.at[1, s]))

    @pl.when(jnp.logical_and(valid, i == 0))
    def _():
        for cp in fetch(be_ref[0], slot):
            cp.start()

    @pl.when(jnp.logical_and(valid, changed))
    def _():
        for cp in fetch(be_ref[i], slot):
            cp.wait()

        @pl.when(nxt_ref[i] >= 0)
        def _():
            for cp in fetch(nxt_ref[i], 1 - slot):
                cp.start()

        w1b_ref[...] = w1f_ref[slot].astype(BF16)
        w2b_ref[...] = w2f_ref[slot].astype(BF16)

    @pl.when(valid)
    def _():
        w = x_ref[...]
        half = w.shape[1]
        x_lo = lax.bitcast_convert_type(w << 16, F32).astype(BF16)
        x_hi = lax.bitcast_convert_type(w & jnp.uint32(0xFFFF0000), F32).astype(BF16)
        h = _dot(x_lo, w1b_ref[0:half, :]) + _dot(x_hi, w1b_ref[half:, :]) + b1_ref[0]
        de = h.shape[1] // 2
        g = jnp.minimum(h[:, :de], SWIGLU_LIMIT)
        lin = jnp.clip(h[:, de:], -SWIGLU_LIMIT, SWIGLU_LIMIT)
        act = g * jax.nn.sigmoid(SWIGLU_ALPHA * g) * (lin + 1.0)
        y = _dot(act.astype(BF16), w2b_ref[...]) + b2_ref[0]
        bits = lax.bitcast_convert_type(y.astype(BF16).astype(F32), jnp.uint32)
        o_ref[...] = (bits[:, :half] >> 16) | (bits[:, half:] & jnp.uint32(0xFFFF0000))

    @pl.when(i >= nv_ref[0])
    def _():
        o_ref[...] = jnp.zeros_like(o_ref)


def moe_experts(xb, block_e, n_valid, next_e, parity, w1, b1, w2, b2, layer):
    p = xb.shape[0]
    depth, ne, d, de2 = w1.shape
    n_blocks = p // MOE_BM
    grid_spec = pltpu.PrefetchScalarGridSpec(
        num_scalar_prefetch=4,
        grid=(n_blocks,),
        in_specs=[
            pl.BlockSpec((MOE_BM, d // 2), lambda i, be, *_: (i, 0)),
            pl.BlockSpec((None, 1, 1, de2), lambda i, be, *_: (layer, be[i], 0, 0)),
            pl.BlockSpec((None, 1, 1, d), lambda i, be, *_: (layer, be[i], 0, 0)),
            pl.BlockSpec(memory_space=pl.ANY),
            pl.BlockSpec(memory_space=pl.ANY),
        ],
        out_specs=pl.BlockSpec((MOE_BM, d // 2), lambda i, be, *_: (i, 0)),
        scratch_shapes=[pltpu.VMEM((2, d, de2), F32), pltpu.VMEM((2, de2 // 2, d), F32),
                        pltpu.VMEM((d, de2), BF16), pltpu.VMEM((de2 // 2, d), BF16),
                        pltpu.SemaphoreType.DMA((2, 2))],
    )
    return pl.pallas_call(
        functools.partial(_expert_kernel, layer),
        grid_spec=grid_spec,
        out_shape=jax.ShapeDtypeStruct((p, d // 2), jnp.uint32),
        compiler_params=_cparams(("arbitrary",)),
        name="moe_experts",
    )(block_e, n_valid, next_e, parity, xb, b1.reshape(depth, ne, 1, de2), b2.reshape(depth, ne, 1, d), w1, w2)


def _combine_kernel(n_experts, src_ref, len_ref, off_ref, dlt_ref, x_ref, mod_ref, g_ref, tw_ref, ti_ref,
                    rk_ref, yblk_ref, o_ref, gbuf, sem):
    j = pl.program_id(0)
    n = pl.num_programs(0)
    slot = j % 2
    tm = x_ref.shape[0]
    rows = gbuf.shape[1]

    def seg_copy(tile, s, e, c):
        base = tile * n_experts + e
        src = pl.multiple_of(src_ref[base] + c * SEG, SEG)
        dst = pl.multiple_of(off_ref[base] + c * SEG, SEG)
        return pltpu.make_async_copy(yblk_ref.at[pl.ds(src, SEG)], gbuf.at[s, pl.ds(dst, SEG)], sem.at[s])

    def for_each_piece(tile, s, fn):
        for e in range(n_experts):
            pieces = (len_ref[tile * n_experts + e] + (SEG - 1)) // SEG

            def body(c, carry, e=e):
                fn(seg_copy(tile, s, e, c))
                return carry
            lax.fori_loop(0, pieces, body, 0)

    @pl.when(j == 0)
    def _():
        gbuf[...] = jnp.zeros_like(gbuf)
        for_each_piece(j, slot, lambda cp: cp.start())

    @pl.when(j + 1 < n)
    def _():
        for_each_piece(j + 1, 1 - slot, lambda cp: cp.start())

    for_each_piece(j, slot, lambda cp: cp.wait())

    ti = ti_ref[...]
    lp = rk_ref[...]
    for e in range(n_experts):
        lp = lp + jnp.where(ti == e, dlt_ref[j * n_experts + e], 0)
    r128 = lax.broadcasted_iota(I32, (LANES, tm), 0)
    lp128 = jnp.zeros((LANES, tm), F32)
    for kk in range(TOP_K):
        lp128 = jnp.where(r128 == kk, lp[kk:kk + 1].astype(F32), lp128)
    lpn = jnp.transpose(lp128)
    tw = tw_ref[...]
    y_lo = jnp.zeros((tm, gbuf.shape[2]), F32)
    y_hi = jnp.zeros((tm, gbuf.shape[2]), F32)
    for c0 in range(0, rows, COMBINE_KC):
        col = (lax.broadcasted_iota(I32, (tm, COMBINE_KC), 1) + c0).astype(F32)
        q = jnp.zeros((tm, COMBINE_KC), F32)
        for kk in range(TOP_K):
            q = q + jnp.where(col == lpn[:, kk:kk + 1], tw[:, kk:kk + 1], 0.0)
        q = q.astype(BF16)
        w = gbuf[slot, c0:c0 + COMBINE_KC, :]
        y_lo = y_lo + _dot(q, lax.bitcast_convert_type(w << 16, F32).astype(BF16))
        y_hi = y_hi + _dot(q, lax.bitcast_convert_type(w & jnp.uint32(0xFFFF0000), F32).astype(BF16))
    y = jnp.concatenate([y_lo, y_hi], axis=1)
    o_ref[...] = x_ref[...] + mod_ref[0, 5:6, :] * _rms(y, g_ref[...])


def moe_combine(seg_src, seg_len, seg_off, seg_dlt, xn, mod, gain, tw, ti_t, rk_t, yblk,
                n_tokens, n_lat, seq_len, n_experts):
    d = xn.shape[1]
    n_lat_tiles = n_lat // TM
    tiles_per_seq = seq_len // TM
    n_tiles = n_tokens // TM
    rows = TM * TOP_K + n_experts * SEG

    def mod_row(j):
        return jnp.where(j < n_lat_tiles, j // tiles_per_seq, 4)

    grid_spec = pltpu.PrefetchScalarGridSpec(
        num_scalar_prefetch=4,
        grid=(n_tiles,),
        in_specs=[
            pl.BlockSpec((TM, d), lambda j, *_: (j, 0)),
            pl.BlockSpec((1, 6, d), lambda j, *_: (mod_row(j), 0, 0)),
            pl.BlockSpec((1, d), lambda j, *_: (0, 0)),
            pl.BlockSpec((TM, LANES), lambda j, *_: (j, 0)),
            pl.BlockSpec((8, TM), lambda j, *_: (0, j)),
            pl.BlockSpec((8, TM), lambda j, *_: (0, j)),
            pl.BlockSpec(memory_space=pl.ANY),
        ],
        out_specs=pl.BlockSpec((TM, d), lambda j, *_: (j, 0)),
        scratch_shapes=[pltpu.VMEM((2, rows, d // 2), jnp.uint32), pltpu.SemaphoreType.DMA((2,))],
    )
    return pl.pallas_call(
        functools.partial(_combine_kernel, n_experts),
        grid_spec=grid_spec,
        out_shape=jax.ShapeDtypeStruct((n_tokens, d), F32),
        compiler_params=_cparams(("arbitrary",)),
        name="moe_combine",
    )(seg_src, seg_len, seg_off, seg_dlt, xn, mod, gain, tw, ti_t, rk_t, yblk)


def _dest_kernel(n_experts, ps_ref, ti_ref, rk_ref, o_ref):
    ti = ti_ref[...]
    dest = rk_ref[...]
    for e in range(n_experts):
        dest = dest + jnp.where(ti == e, ps_ref[e], 0)
    o_ref[0] = dest


def moe_dest(pstart, ti_t, rk_t):
    n_tokens = ti_t.shape[1]
    lanes = pl.BlockSpec((8, TM), lambda j, ps: (0, j))
    dest = pl.pallas_call(
        functools.partial(_dest_kernel, pstart.shape[0]),
        grid_spec=pltpu.PrefetchScalarGridSpec(
            num_scalar_prefetch=1, grid=(n_tokens // TM,), in_specs=[lanes, lanes],
            out_specs=pl.BlockSpec((1, 8, TM), lambda j, ps: (j, 0, 0))),
        out_shape=jax.ShapeDtypeStruct((n_tokens // TM, 8, TM), I32),
        compiler_params=_cparams(("arbitrary",)),
        name="moe_dest",
    )(pstart, ti_t, rk_t)
    return dest.reshape(-1)


def _dispatch_kernel(dest_ref, hf_ref, xb_in_ref, xb_ref, sem):
    del xb_in_ref
    tm = hf_ref.shape[0]

    def row_copy(t, kk):
        return pltpu.make_async_copy(hf_ref.at[pl.ds(t, 1)], xb_ref.at[pl.ds(dest_ref[kk * tm + t], 1)], sem)

    def start(i, carry):
        t0 = pl.multiple_of(i * DMA_UNROLL, DMA_UNROLL)
        for u in range(DMA_UNROLL):
            for kk in range(TOP_K):
                row_copy(t0 + u, kk).start()
        return carry
    lax.fori_loop(0, tm // DMA_UNROLL, start, 0)

    def wait(i, carry):
        t0 = pl.multiple_of(i * DMA_UNROLL, DMA_UNROLL)
        for u in range(DMA_UNROLL):
            for kk in range(TOP_K):
                row_copy(t0 + u, kk).wait()
        return carry
    lax.fori_loop(0, tm // DMA_UNROLL, wait, 0)


def moe_dispatch(dest_t, hfp, n_rows):
    n_tokens, w = hfp.shape
    xb0 = jnp.zeros((n_rows, w), hfp.dtype)
    return pl.pallas_call(
        _dispatch_kernel,
        grid=(n_tokens // TM,),
        in_specs=[
            pl.BlockSpec((8 * TM,), lambda j: (j,), memory_space=pltpu.SMEM),
            pl.BlockSpec((TM, w), lambda j: (j, 0)),
            pl.BlockSpec(memory_space=pl.ANY),
        ],
        out_specs=pl.BlockSpec(memory_space=pl.ANY),
        out_shape=jax.ShapeDtypeStruct((n_rows, w), hfp.dtype),
        input_output_aliases={2: 0},
        scratch_shapes=[pltpu.SemaphoreType.DMA(())],
        compiler_params=_cparams(("arbitrary",)),
        name="moe_dispatch",
    )(dest_t, hfp, xb0)


def kernel(x, c, ctx, c_ctx, ada_w, ada_b, norm_mix_pre, norm_mix_post, norm_ffn_pre, norm_ffn_post,
           w_in, w_branch_gate, b_branch_gate, w_branch, w_out, pool_w, pool_scale,
           ssm_a_re, ssm_a_im, ssm_log_dt, ssm_b_re, ssm_b_im, ssm_c_re, ssm_c_im, ssm_d, ssm_w_glu,
           gmlp_ln_g, gmlp_ln_b, gmlp_w_s, gmlp_b_s, na_rpb,
           router_w, router_b, expert_w1, expert_b1, expert_w2, expert_b2):
    nb, seq_len, d = x.shape
    ctx_len = ctx.shape[1]
    depth = ada_w.shape[0]
    n_lat = nb * seq_len
    n_all = n_lat + nb * ctx_len
    n_experts = router_w.shape[-1]
    assert nb == 4 and seq_len % TM == 0 and (nb * ctx_len) % TM == 0 and ctx_len % TP == 0

    xs = jnp.concatenate([x.reshape(n_lat, d), ctx.reshape(nb * ctx_len, d)], axis=0)
    cvec = jnp.concatenate([c, c_ctx[None, :], jnp.zeros((3, d), F32)], axis=0)
    mods = ada_modulation(cvec, ada_w, ada_b).reshape(depth, 8, 6, d)
    cos_t, sin_t = rope_tables(seq_len)
    kflat, sall, wst, dec = jax.vmap(s5_tables)(ssm_a_re, ssm_a_im, ssm_log_dt, ssm_b_re, ssm_b_im,
                                                ssm_c_re, ssm_c_im)
    na_tab = jax.vmap(na_bias_table)(na_rpb)

    out = None
    for l in range(depth):
        last = l == depth - 1
        mod = mods[l]
        row = lambda v: v.reshape(1, -1)
        pa, pu, pg, q, k, v = premix(xs, mod, row(norm_mix_pre[l]), w_in[l].astype(BF16), cos_t, sin_t,
                                     n_lat, seq_len)

        n_mix = n_lat if last else n_all
        w_bd = jax.scipy.linalg.block_diag(*[pool_w[l, g] for g in range(POOL_GROUPS)]).astype(BF16)
        ya = pool_mix(pa, w_bd, row(pool_scale[l]), n_mix, n_lat, seq_len, ctx_len)

        bs_full = jnp.repeat(gmlp_b_s[l].T, 256 // GMLP_HEADS, axis=1)
        yg = gmlp_mix(pg, row(gmlp_ln_g[l]), row(gmlp_ln_b[l]), gmlp_w_s[l].astype(BF16), bs_full, n_mix)

        y_g = s5_core(s5_to_groups(pu), kflat, sall, wst, dec, l, ctx_len // SSM_T, seq_len // SSM_T, nb)
        yb = s5_post(s5_from_groups(y_g), pu, row(ssm_d[l]), ssm_w_glu[l].astype(BF16), n_mix)

        yd = na_attention(q, k, v, na_tab, l, nb, seq_len, ctx_len, not last)

        rw = jnp.pad(router_w[l], ((0, 0), (0, LANES - n_experts)))
        rw_hi = rw.astype(BF16)
        rw_lo = (rw - rw_hi.astype(F32)).astype(BF16)
        rb = jnp.pad(router_b[l], (0, LANES - n_experts), constant_values=NEG).reshape(1, LANES)
        n_tok = n_lat if last else n_all
        xn, hfp, ti_t, rk_t, tw, counts, tcnt = merge_and_route(
            xs, mod, row(norm_mix_pre[l]), row(norm_mix_post[l]), row(norm_ffn_pre[l]),
            ya, yb, yg, yd, w_branch_gate[l].astype(BF16), row(b_branch_gate[l]),
            w_branch[l].astype(BF16), w_out[l].astype(BF16), rw_hi, rw_lo, rb, n_tok, n_lat, seq_len, n_experts)

        cnt = counts[:, 0].astype(I32)
        padded = (cnt + MOE_BM - 1) // MOE_BM * MOE_BM
        pend = jnp.cumsum(padded)
        n_blocks = (n_tok * TOP_K + (n_tok // TM) * n_experts * (SEG - 1)) // MOE_BM + 1 + n_experts
        blk_start = jnp.arange(n_blocks, dtype=I32) * MOE_BM
        block_e = jnp.minimum(jnp.sum((pend[None, :] <= blk_start[:, None]).astype(I32), axis=1), n_experts - 1)
        n_valid = (pend[-1] // MOE_BM).astype(I32).reshape(1)
        eidx = jnp.arange(n_experts, dtype=I32)
        nonempty = padded > 0
        later = (eidx[None, :] > eidx[:, None]) & nonempty[None, :]
        nxt_e = jnp.min(jnp.where(later, eidx[None, :], n_experts), axis=1)
        nxt_e = jnp.where(nxt_e >= n_experts, -1, nxt_e)
        par_e = (jnp.cumsum(nonempty.astype(I32)) - 1) % 2
        pick = (block_e[:, None] == eidx[None, :]).astype(I32)
        next_e = jnp.sum(pick * nxt_e[None, :], axis=1)
        parity = jnp.sum(pick * par_e[None, :], axis=1)

        dest_t = moe_dest(pend - padded, ti_t, rk_t)
        xb = moe_dispatch(dest_t, hfp, n_blocks * MOE_BM)
        yblk = moe_experts(xb, block_e, n_valid, next_e, parity, expert_w1, expert_b1, expert_w2, expert_b2, l)
        tc = tcnt[:, :, 0].astype(I32)
        c8 = (tc + SEG - 1) // SEG * SEG
        run = jnp.cumsum(c8, axis=0) - c8
        seg_off = jnp.cumsum(c8, axis=1) - c8
        seg_src = (pend - padded)[None, :] + run
        xs = moe_combine(seg_src.reshape(-1), tc.reshape(-1), seg_off.reshape(-1), (seg_off - run).reshape(-1),
                         xn, mod, row(norm_ffn_post[l]), tw, ti_t, rk_t, yblk, n_tok, n_lat, seq_len, n_experts)
        out = xs
    return out.reshape(nb, seq_len, d)
```

```python
import functools
import math

import jax
import jax.numpy as jnp
import jax.scipy.linalg
from jax import lax
from jax.experimental import pallas as pl
from jax.experimental.pallas import tpu as pltpu

F32 = jnp.float32
BF16 = jnp.bfloat16
I32 = jnp.int32

GRID_W = 64
EPS = 1e-6
POOL_GROUPS = 4
POOL_WINDOWS = (2, 4, 8, 16)
SSM_CH = 16
SSM_N = 64
GMLP_HEADS = 4
CHUNK = 128
NA_HEADS = 4
HEAD_DIM = 64
WIN_ROWS = 8
WIN_COLS = 16
ROPE_BASE = 10000.0
BRANCH_W = 256
N_BRANCH = 4
TOP_K = 4
SWIGLU_LIMIT = 7.0
SWIGLU_ALPHA = 1.702

TM = 512
TP = 256
POOL_HALO = 16
SSM_T = 32
MOE_BM = 512
DMA_UNROLL = 8
SEG = 8
COMBINE_KC = 256
LANES = 128
NEG = -1e30
VMEM_LIMIT = 56 * 1024 * 1024


def _cparams(sem):
    return pltpu.CompilerParams(dimension_semantics=sem, vmem_limit_bytes=VMEM_LIMIT)


def _dot(a, b):
    return jnp.dot(a, b, preferred_element_type=F32)


def _dot_nt(a, b):
    return lax.dot_general(a, b, (((1,), (1,)), ((), ())), preferred_element_type=F32)


def _ada_kernel(c_ref, w_ref, b_ref, o_ref):
    c = c_ref[...]
    s = c * jax.nn.sigmoid(c)
    o_ref[0] = jnp.dot(s, w_ref[0], preferred_element_type=F32,
                       precision=lax.Precision.HIGHEST) + b_ref[0]


def ada_modulation(cvec, ada_w, ada_b):
    depth, d, n = ada_w.shape
    tn = 1536
    return pl.pallas_call(
        _ada_kernel,
        grid=(depth, n // tn),
        in_specs=[
            pl.BlockSpec((8, d), lambda l, j: (0, 0)),
            pl.BlockSpec((1, d, tn), lambda l, j: (l, 0, j)),
            pl.BlockSpec((1, 1, tn), lambda l, j: (l, 0, j)),
        ],
        out_specs=pl.BlockSpec((1, 8, tn), lambda l, j: (l, 0, j)),
        out_shape=jax.ShapeDtypeStruct((depth, 8, n), F32),
        compiler_params=_cparams(("arbitrary", "arbitrary")),
        name="ada_modulation",
    )(cvec, ada_w, ada_b.reshape(depth, 1, n))


def _rms(x, g):
    return x * lax.rsqrt(jnp.mean(x * x, axis=-1, keepdims=True) + EPS) * g


def _premix_kernel(n_lat_tiles, x_ref, mod_ref, g_ref, w_ref, cos_ref, sin_ref,
                   pa_ref, pu_ref, pg_ref, q_ref, k_ref, v_ref):
    j = pl.program_id(0)
    x = x_ref[...]
    h = _rms(x, g_ref[...])
    h = h * (1.0 + mod_ref[0, 1:2, :]) + mod_ref[0, 0:1, :]
    p = _dot(h.astype(BF16), w_ref[...])
    wa = p.shape[1] - 768
    pa_ref[...] = p[:, 0:256]
    pu_ref[...] = p[:, 256:512]
    pg_ref[...] = p[:, 512:wa]
    q = p[:, wa:wa + 256]
    k = p[:, wa + 256:wa + 512]
    v = p[:, wa + 512:wa + 768]
    is_lat = j < n_lat_tiles
    cos = jnp.where(is_lat, cos_ref[...], 1.0)
    sin = jnp.where(is_lat, sin_ref[...], 0.0)
    lane = lax.broadcasted_iota(I32, q.shape, 1)
    first = (lane % 32) < 16

    def rope(t):
        partner = jnp.where(first, pltpu.roll(t, 256 - 16, axis=1), pltpu.roll(t, 16, axis=1))
        return t * cos + partner * sin

    q_ref[...] = (rope(q) * (HEAD_DIM ** -0.5)).astype(BF16)
    k_ref[...] = rope(k).astype(BF16)
    v_ref[...] = v.astype(BF16)


def premix(x, mod, gain, w_in_bf, cos_t, sin_t, n_lat, seq_len):
    nt, d = x.shape
    in_w = w_in_bf.shape[1]
    wa = in_w - 768
    n_lat_tiles = n_lat // TM
    tiles_per_seq = seq_len // TM

    def mod_row(j):
        return jnp.where(j < n_lat_tiles, j // tiles_per_seq, 4)

    def tab_blk(j):
        return jnp.where(j < n_lat_tiles, j % tiles_per_seq, 0)

    return pl.pallas_call(
        functools.partial(_premix_kernel, n_lat_tiles),
        grid=(nt // TM,),
        in_specs=[
            pl.BlockSpec((TM, d), lambda j: (j, 0)),
            pl.BlockSpec((1, 6, d), lambda j: (mod_row(j), 0, 0)),
            pl.BlockSpec((1, d), lambda j: (0, 0)),
            pl.BlockSpec((d, in_w), lambda j: (0, 0)),
            pl.BlockSpec((TM, 256), lambda j: (tab_blk(j), 0)),
            pl.BlockSpec((TM, 256), lambda j: (tab_blk(j), 0)),
        ],
        out_specs=[
            pl.BlockSpec((TM, 256), lambda j: (j, 0)),
            pl.BlockSpec((TM, 256), lambda j: (j, 0)),
            pl.BlockSpec((TM, wa - 512), lambda j: (j, 0)),
            pl.BlockSpec((TM, 256), lambda j: (j, 0)),
            pl.BlockSpec((TM, 256), lambda j: (j, 0)),
            pl.BlockSpec((TM, 256), lambda j: (j, 0)),
        ],
        out_shape=[
            jax.ShapeDtypeStruct((nt, 256), F32),
            jax.ShapeDtypeStruct((nt, 256), F32),
            jax.ShapeDtypeStruct((nt, wa - 512), F32),
            jax.ShapeDtypeStruct((nt, 256), BF16),
            jax.ShapeDtypeStruct((nt, 256), BF16),
            jax.ShapeDtypeStruct((nt, 256), BF16),
        ],
        compiler_params=_cparams(("arbitrary",)),
        name="premix",
    )(x, mod, gain, w_in_bf, cos_t, sin_t)


def rope_tables(seq_len):
    t = jnp.arange(seq_len)
    half = HEAD_DIM // 2
    nf = half // 2
    inv = ROPE_BASE ** (-jnp.arange(nf, dtype=F32) / nf)
    d = jnp.arange(HEAD_DIM)
    pos = jnp.where((d // half)[None, :] == 0, (t // GRID_W)[:, None], (t % GRID_W)[:, None]).astype(F32)
    ang = pos * inv[(d % half) % nf][None, :]
    sign = jnp.where((d % half) < nf, -1.0, 1.0)[None, :]
    cos = jnp.tile(jnp.cos(ang), (1, NA_HEADS))
    sin = jnp.tile(jnp.sin(ang) * sign, (1, NA_HEADS))
    return cos.astype(F32), sin.astype(F32)


def _pool_kernel(n_lat_tiles, tiles_per_seq, ctx_tiles_per_seq, cur_ref, prev_ref, next_ref,
                 w_ref, sc_ref, o_ref, ext_ref):
    j = pl.program_id(0)
    is_lat = j < n_lat_tiles
    t0 = jnp.where(is_lat, (j % tiles_per_seq) * TP, ((j - n_lat_tiles) % ctx_tiles_per_seq) * TP)
    slen = jnp.where(is_lat, tiles_per_seq * TP, ctx_tiles_per_seq * TP)
    u = cur_ref[...]
    h = POOL_HALO
    ext_ref[0:h, :] = jnp.where(t0 > 0, prev_ref[...], 0.0)
    ext_ref[h:h + TP, :] = u
    ext_ref[h + TP:h + TP + h, :] = jnp.where(t0 + TP < slen, next_ref[...], 0.0)
    t = t0 + lax.broadcasted_iota(I32, (TP, 1), 0)
    lane = lax.broadcasted_iota(I32, (TP, u.shape[1]), 1)
    grp = lane // (u.shape[1] // POOL_GROUPS)
    mean = jnp.zeros_like(u)
    for gi, w in enumerate(POOL_WINDOWS):
        acc = ext_ref[pl.ds(h - w // 2, TP), :]
        for o in range(1, w):
            acc = acc + ext_ref[pl.ds(h - w // 2 + o, TP), :]
        cnt = (jnp.minimum(t - w // 2 + w, slen) - jnp.maximum(t - w // 2, 0)).astype(F32)
        mean = jnp.where(grp == gi, acc / cnt, mean)
    y = _dot((mean - u).astype(BF16), w_ref[...]) * sc_ref[...]
    o_ref[...] = y.astype(BF16)


def pool_mix(p1, w_bd_bf, scale, n_rows, n_lat, seq_len, ctx_len):
    nt = n_rows
    hb = TP // POOL_HALO
    n_halo_blocks = p1.shape[0] // POOL_HALO
    return pl.pallas_call(
        functools.partial(_pool_kernel, n_lat // TP, seq_len // TP, ctx_len // TP),
        grid=(nt // TP,),
        in_specs=[
            pl.BlockSpec((TP, 256), lambda j: (j, 0)),
            pl.BlockSpec((POOL_HALO, 256), lambda j: (jnp.maximum(j * hb - 1, 0), 0)),
            pl.BlockSpec((POOL_HALO, 256), lambda j: (jnp.minimum((j + 1) * hb, n_halo_blocks - 1), 0)),
            pl.BlockSpec((256, 256), lambda j: (0, 0)),
            pl.BlockSpec((1, 256), lambda j: (0, 0)),
        ],
        out_specs=pl.BlockSpec((TP, 256), lambda j: (j, 0)),
        out_shape=jax.ShapeDtypeStruct((n_rows, 256), BF16),
        scratch_shapes=[pltpu.VMEM((TP + 2 * POOL_HALO, 256), F32)],
        compiler_params=_cparams(("arbitrary",)),
        name="pool_mix",
    )(p1, p1, p1, w_bd_bf, scale)


def _gmlp_kernel(p_ref, g_ref, b_ref, ws_ref, bs_ref, o_ref):
    uv = jax.nn.gelu(p_ref[...])
    w = uv.shape[1] // 2
    u = uv[:, :w]
    v = uv[:, w:]
    mu = jnp.mean(v, axis=-1, keepdims=True)
    var = jnp.mean((v - mu) ** 2, axis=-1, keepdims=True)
    v = ((v - mu) * lax.rsqrt(var + EPS) * g_ref[...] + b_ref[...]).astype(BF16)
    lane = lax.broadcasted_iota(I32, (CHUNK, w), 1)
    head = lane // (w // GMLP_HEADS)
    for c in range(TP // CHUNK):
        vc = v[c * CHUNK:(c + 1) * CHUNK]
        sv = bs_ref[...]
        for hh in range(GMLP_HEADS):
            sv = sv + jnp.where(head == hh, _dot(ws_ref[hh], vc), 0.0)
        o_ref[c * CHUNK:(c + 1) * CHUNK, :] = (u[c * CHUNK:(c + 1) * CHUNK] * sv).astype(BF16)


def gmlp_mix(p1, ln_g, ln_b, ws_bf, bs_full, n_rows):
    return pl.pallas_call(
        _gmlp_kernel,
        grid=(n_rows // TP,),
        in_specs=[
            pl.BlockSpec((TP, 512), lambda j: (j, 0)),
            pl.BlockSpec((1, 256), lambda j: (0, 0)),
            pl.BlockSpec((1, 256), lambda j: (0, 0)),
            pl.BlockSpec((GMLP_HEADS, CHUNK, CHUNK), lambda j: (0, 0, 0)),
            pl.BlockSpec((CHUNK, 256), lambda j: (0, 0)),
        ],
        out_specs=pl.BlockSpec((TP, 256), lambda j: (j, 0)),
        out_shape=jax.ShapeDtypeStruct((n_rows, 256), BF16),
        compiler_params=_cparams(("arbitrary",)),
        name="gmlp_mix",
    )(p1, ln_g, ln_b, ws_bf, bs_full)


def s5_tables(a_re, a_im, log_dt, b_re, b_im, c_re, c_im):
    hp = lax.Precision.HIGHEST
    T = SSM_T
    a_re = a_re.astype(F32)
    a_im = a_im.astype(F32)
    dt = jnp.exp(log_dt.astype(F32))[..., None]
    kk = jnp.arange(T + 1, dtype=F32)[:, None, None, None]
    mag = jnp.exp(kk * (a_re * dt)[None])
    pw_re = mag * jnp.cos(kk * (a_im * dt)[None])
    pw_im = mag * jnp.sin(kk * (a_im * dt)[None])

    def cmul(xr, xi, yr, yi):
        return xr * yr - xi * yi, xr * yi + xi * yr

    nr, ni = pw_re[1] - 1.0, pw_im[1]
    den = a_re * a_re + a_im * a_im
    zr, zi = (nr * a_re + ni * a_im) / den, (ni * a_re - nr * a_im) / den
    bb_re, bb_im = cmul(zr[..., None], zi[..., None], b_re.astype(F32), b_im.astype(F32))
    c_re = c_re.astype(F32)
    c_im = c_im.astype(F32)
    G, P = c_re.shape[1], c_re.shape[2]
    e_re, e_im = cmul(c_re[None], c_im[None], pw_re[:T, :, :, None, :], pw_im[:T, :, :, None, :])
    kern = (jnp.einsum('kdgpn,dgnq->dgqpk', e_re, bb_re, precision=hp)
            - jnp.einsum('kdgpn,dgnq->dgqpk', e_im, bb_im, precision=hp))
    kflat = jnp.stack([kern[0], kern[1][..., ::-1]], axis=1).reshape(G, 2, P, P * T)
    s_idx = jnp.arange(T)
    def pack_s(pr, pi, d):
        zr_, zi_ = cmul(pr[..., None], pi[..., None], bb_re[d][None], bb_im[d][None])
        f = lambda z: z.transpose(1, 3, 0, 2).reshape(G, P * T, -1)
        return f(zr_), f(zi_)

    sfr, sfi = pack_s(pw_re[T - 1 - s_idx, 0], pw_im[T - 1 - s_idx, 0], 0)
    sbr, sbi = pack_s(pw_re[s_idx, 1], pw_im[s_idx, 1], 1)
    sall = jnp.concatenate([sfr, sfi, sfi, sfr, sbr, sbi, sbi, sbr], axis=-1)
    def pack_w(pr, pi, d):
        zr_, zi_ = cmul(c_re[d][:, :, :, None], c_im[d][:, :, :, None],
                        pr.transpose(1, 2, 0)[:, None], pi.transpose(1, 2, 0)[:, None])
        f = lambda z: z.transpose(0, 2, 1, 3).reshape(G, -1, P * T)
        return jnp.concatenate([f(zr_), -f(zi_)], axis=1)

    wst = jnp.concatenate([pack_w(pw_re[1 + s_idx, 0], pw_im[1 + s_idx, 0], 0),
                           pack_w(pw_re[T - s_idx, 1], pw_im[T - s_idx, 1], 1)], axis=1)
    rows = []
    for dd in range(2):
        ar, ai = pw_re[T, dd], pw_im[T, dd]
        rows += [jnp.concatenate([ar, ar], -1), jnp.concatenate([-ai, ai], -1), jnp.concatenate([ai, -ai], -1)]
    rows += [jnp.zeros_like(rows[0])] * 2
    dec = jnp.stack(rows, axis=1)
    return kflat.astype(F32), sall.astype(BF16), wst.astype(BF16), dec.astype(F32)


def _s5_kernel(n_ctx_chunks, n_lat_chunks, nb, a_ref, kflat_ref, sall_ref, wst_ref, dec_ref,
               o_ref, x_ref, h_ref, toe_ref):
    T = SSM_T
    kf = kflat_ref[0, 0]
    kb = kflat_ref[0, 1]
    w = kf.shape[1]
    tpos = lax.broadcasted_iota(I32, kf.shape, 1) % T
    for s in range(T):
        row = (jnp.where(tpos >= s, pltpu.roll(kf, s, axis=1), 0.0)
               + jnp.where(tpos <= s, pltpu.roll(kb, (w - (T - 1 - s)) % w, axis=1), 0.0))
        for c in range(w // LANES):
            toe_ref[c, pl.ds(s, SSM_CH, stride=T), :] = row[:, c * LANES:(c + 1) * LANES]
    a = a_ref[0].astype(BF16)
    y = jnp.concatenate([_dot(a, toe_ref[c].astype(BF16)) for c in range(w // LANES)], axis=1)
    x = _dot(a, sall_ref[0])
    for i in range(4):
        x_ref[i] = x[:, i * LANES:(i + 1) * LANES]
    dec = dec_ref[0]
    n_chunks = n_ctx_chunks + n_lat_chunks
    fwd_order = list(range(n_lat_chunks, n_chunks)) + list(range(n_lat_chunks))
    bwd_order = list(range(n_chunks - 1, n_lat_chunks - 1, -1)) + list(range(n_lat_chunks - 1, -1, -1))
    for d, order in enumerate((fwd_order, bwd_order)):
        a1 = dec[3 * d:3 * d + 1]
        a2 = dec[3 * d + 1:3 * d + 2]
        a3 = dec[3 * d + 2:3 * d + 3]
        s1 = jnp.zeros((nb, LANES), F32)
        s2 = jnp.zeros((nb, LANES), F32)
        for c in order:
            if c < n_lat_chunks:
                r = pl.ds(c, nb, stride=n_lat_chunks)
            else:
                r = pl.ds(nb * n_lat_chunks + c - n_lat_chunks, nb, stride=n_ctx_chunks)
            h_ref[d, r, :] = s1
            x1 = x_ref[2 * d, r, :]
            x2 = x_ref[2 * d + 1, r, :]
            s1, s2 = a1 * s1 + a2 * s2 + x1, a1 * s2 + a3 * s1 + x2
    for d in range(2):
        y = y + _dot(h_ref[d].astype(BF16), wst_ref[0, d * LANES:(d + 1) * LANES, :])
    o_ref[0] = y


def s5_core(a_g, kflat, sall, wst, dec, layer, n_ctx_chunks, n_lat_chunks, nb):
    G, rows, w = a_g.shape
    return pl.pallas_call(
        functools.partial(_s5_kernel, n_ctx_chunks, n_lat_chunks, nb),
        grid=(G,),
        in_specs=[
            pl.BlockSpec((1, rows, w), lambda g: (g, 0, 0)),
            pl.BlockSpec((None, 1, 2, SSM_CH, w), lambda g: (layer, g, 0, 0, 0)),
            pl.BlockSpec((None, 1, w, 512), lambda g: (layer, g, 0, 0)),
            pl.BlockSpec((None, 1, 256, w), lambda g: (layer, g, 0, 0)),
            pl.BlockSpec((None, 1, 8, LANES), lambda g: (layer, g, 0, 0)),
        ],
        out_specs=pl.BlockSpec((1, rows, w), lambda g: (g, 0, 0)),
        out_shape=jax.ShapeDtypeStruct((G, rows, w), F32),
        scratch_shapes=[pltpu.VMEM((4, rows, LANES), F32), pltpu.VMEM((2, rows, LANES), F32),
                        pltpu.VMEM((w // LANES, w, LANES), F32)],
        compiler_params=_cparams(("arbitrary",)),
        name="s5_core",
    )(a_g, kflat, sall, wst, dec)


def s5_to_groups(u):
    T = SSM_T
    G = u.shape[1] // SSM_CH
    return u.reshape(-1, T, G, SSM_CH).transpose(2, 0, 3, 1).reshape(G, -1, SSM_CH * T)


def s5_from_groups(y):
    T = SSM_T
    G = y.shape[0]
    return y.reshape(G, -1, SSM_CH, T).transpose(1, 3, 0, 2).reshape(-1, G * SSM_CH)


def _s5_post_kernel(y_ref, u_ref, d_ref, w_ref, o_ref):
    y = jax.nn.gelu(y_ref[...] + d_ref[...] * u_ref[...])
    z = _dot(y.astype(BF16), w_ref[...])
    w = z.shape[1] // 2
    o_ref[...] = (z[:, :w] * jax.nn.sigmoid(z[:, w:])).astype(BF16)


def s5_post(y_nat, pu, d, w_glu_bf, n_rows):
    nt = n_rows
    return pl.pallas_call(
        _s5_post_kernel,
        grid=(nt // TM,),
        in_specs=[
            pl.BlockSpec((TM, 256), lambda j: (j, 0)),
            pl.BlockSpec((TM, 256), lambda j: (j, 0)),
            pl.BlockSpec((1, 256), lambda j: (0, 0)),
            pl.BlockSpec((256, 512), lambda j: (0, 0)),
        ],
        out_specs=pl.BlockSpec((TM, 256), lambda j: (j, 0)),
        out_shape=jax.ShapeDtypeStruct((nt, 256), BF16),
        compiler_params=_cparams(("arbitrary",)),
        name="s5_post",
    )(y_nat, pu, d, w_glu_bf)


def na_bias_table(rpb):
    ci = jnp.arange(GRID_W)
    cs = jnp.clip(ci - WIN_COLS // 2, 0, GRID_W - WIN_COLS)
    kc = jnp.arange(GRID_W)
    inside = (kc[None, :] >= cs[:, None]) & (kc[None, :] < cs[:, None] + WIN_COLS)
    col_off = kc[None, :] - ci[:, None] + (WIN_COLS - 1)
    sel = (col_off[:, :, None] == jnp.arange(2 * WIN_COLS - 1)[None, None, :]).astype(F32)
    base = jnp.einsum('hrw,qkw->hrqk', rpb.astype(F32), sel, precision=lax.Precision.HIGHEST)
    base = jnp.where(inside[None, None], base, NEG)
    tab = jnp.stack([base[:, WIN_ROWS - 1 - ty:2 * WIN_ROWS - 1 - ty] for ty in range(WIN_ROWS)], axis=0)
    tab = tab.transpose(0, 1, 3, 2, 4)
    return tab.reshape(WIN_ROWS, NA_HEADS * GRID_W, WIN_ROWS * GRID_W)


def _lane_blocks(x, op):
    out = x[:, 0:LANES]
    for i in range(1, x.shape[1] // LANES):
        out = op(out, x[:, i * LANES:(i + 1) * LANES])
    return out


def _heads_stack(q):
    lane = lax.broadcasted_iota(I32, q.shape, 1)
    zero = jnp.zeros_like(q)
    return jnp.concatenate([jnp.where(lane // HEAD_DIM == hh, q, zero) for hh in range(NA_HEADS)], axis=0)


def _heads_merge(o4, m):
    lane = lax.broadcasted_iota(I32, (m, o4.shape[1]), 1)
    out = jnp.zeros((m, o4.shape[1]), F32)
    for hh in range(NA_HEADS):
        out = jnp.where(lane // HEAD_DIM == hh, o4[hh * m:(hh + 1) * m], out)
    return out


def _na_kernel(rows_per_step, n_rows, lat_steps, q_ref, k_ref, v_ref, kc_ref, vc_ref, tab_ref, o_ref):
    i = pl.program_id(1)
    kc = kc_ref[...]
    vc = vc_ref[...]
    nkeys = WIN_ROWS * GRID_W

    @pl.when(i < lat_steps)
    def _():
        def body(a, carry):
            r = i * rows_per_step + a
            rs = jnp.clip(r - WIN_ROWS // 2, 0, n_rows - WIN_ROWS)
            ty = r - rs
            q0 = pl.multiple_of(a * GRID_W, GRID_W)
            k0 = pl.multiple_of(rs * GRID_W, GRID_W)
            q4 = _heads_stack(q_ref[pl.ds(q0, GRID_W), :])
            kw = k_ref[pl.ds(k0, nkeys), :]
            vw = v_ref[pl.ds(k0, nkeys), :]
            s = _dot_nt(q4, kw) + tab_ref[ty]
            sc = _dot_nt(q4, kc)
            m = jnp.max(jnp.maximum(_lane_blocks(s, jnp.maximum), _lane_blocks(sc, jnp.maximum)),
                        axis=-1, keepdims=True)
            e = jnp.exp(s - m)
            ec = jnp.exp(sc - m)
            den = jnp.sum(_lane_blocks(e, jnp.add) + _lane_blocks(ec, jnp.add), axis=-1, keepdims=True)
            o4 = (_dot(e.astype(BF16), vw) + _dot(ec.astype(BF16), vc)) / den
            o_ref[pl.ds(q0, GRID_W), :] = _heads_merge(o4, GRID_W).astype(BF16)
            return carry

        lax.fori_loop(0, rows_per_step, body, 0, unroll=True)

    @pl.when(i >= lat_steps)
    def _():
        q4 = _heads_stack(q_ref[...])
        s = _dot_nt(q4, kc)
        m = jnp.max(_lane_blocks(s, jnp.maximum), axis=-1, keepdims=True)
        e = jnp.exp(s - m)
        den = jnp.sum(_lane_blocks(e, jnp.add), axis=-1, keepdims=True)
        o4 = _dot(e.astype(BF16), vc) / den
        o_ref[...] = _heads_merge(o4, q_ref.shape[0]).astype(BF16)


def na_attention(q, k, v, tab, layer, nb, seq_len, ctx_len, with_ctx):
    tq = ctx_len
    rows_per_step = tq // GRID_W
    n_rows = seq_len // GRID_W
    lat_steps = seq_len // tq
    ctx_blk0 = nb * seq_len // ctx_len

    def q_blk(b, i):
        return (jnp.where(i < lat_steps, b * lat_steps + i, ctx_blk0 + b), 0)

    n_out = nb * seq_len + (nb * ctx_len if with_ctx else 0)
    return pl.pallas_call(
        functools.partial(_na_kernel, rows_per_step, n_rows, lat_steps),
        grid=(nb, lat_steps + (1 if with_ctx else 0)),
        in_specs=[
            pl.BlockSpec((tq, 256), q_blk),
            pl.BlockSpec((seq_len, 256), lambda b, i: (b, 0)),
            pl.BlockSpec((seq_len, 256), lambda b, i: (b, 0)),
            pl.BlockSpec((ctx_len, 256), lambda b, i: (ctx_blk0 + b, 0)),
            pl.BlockSpec((ctx_len, 256), lambda b, i: (ctx_blk0 + b, 0)),
            pl.BlockSpec((None,) + tab.shape[1:], lambda b, i: (layer, 0, 0, 0)),
        ],
        out_specs=pl.BlockSpec((tq, 256), q_blk),
        out_shape=jax.ShapeDtypeStruct((n_out, 256), BF16),
        compiler_params=_cparams(("arbitrary", "arbitrary")),
        name="na_attention",
    )(q, k, v, k, v, tab)


def _merge_kernel(tiles_per_seq, n_lat_tiles, x_ref, mod_ref, gpre_ref, gpost_ref, gffn_ref,
                  ya_ref, yb_ref, yg_ref, yd_ref, wbg_ref, bbg_ref, wbr_ref, wout_ref,
                  wrh_ref, wrl_ref, br_ref, tri_ref, xn_ref, hf_ref, ti_ref, rk_ref, tw_ref, cnt_ref,
                  tcnt_ref, run_ref):
    x = x_ref[...]
    d = x.shape[1]
    mod = mod_ref[0]
    h = _rms(x, gpre_ref[...]) * (1.0 + mod[1:2]) + mod[0:1]
    hb = h.astype(BF16)
    acc = jnp.zeros_like(x)
    for bi, br_ref_i in enumerate((ya_ref, yb_ref, yg_ref, yd_ref)):
        gate = jax.nn.sigmoid(_dot(hb, wbg_ref[:, bi * d:(bi + 1) * d]) + bbg_ref[:, bi * d:(bi + 1) * d])
        acc = acc + gate * _dot(br_ref_i[...], wbr_ref[bi])
    y = _dot(acc.astype(BF16), wout_ref[...])
    xn = x + mod[2:3] * _rms(y, gpost_ref[...])
    xn_ref[...] = xn
    hf = _rms(xn, gffn_ref[...]) * (1.0 + mod[4:5]) + mod[3:4]
    hf_hi = hf.astype(BF16)
    bits = lax.bitcast_convert_type(hf_hi.astype(F32), jnp.uint32)
    hf_ref[...] = (bits[:, :d // 2] >> 16) | (bits[:, d // 2:] & jnp.uint32(0xFFFF0000))
    hf_lo = (hf - hf_hi.astype(F32)).astype(BF16)
    logits = _dot(hf_hi, wrh_ref[...]) + (_dot(hf_hi, wrl_ref[...]) + _dot(hf_lo, wrh_ref[...])) + br_ref[...]

    @pl.when(pl.program_id(0) == 0)
    def _():
        run_ref[...] = jnp.zeros_like(run_ref)

    ne = run_ref.shape[0]
    tm = logits.shape[0]
    work = jnp.transpose(logits)[:ne]
    erow = lax.broadcasted_iota(I32, work.shape, 0)
    idxs, es, ohs = [], [], []
    v0 = None
    for kk in range(TOP_K):
        m = jnp.max(work, axis=0, keepdims=True)
        idx = jnp.min(jnp.where(work == m, erow, ne), axis=0, keepdims=True)
        oh = erow == idx
        if kk == 0:
            v0 = m
            es.append(jnp.ones_like(m))
        else:
            es.append(jnp.exp(m - v0))
        idxs.append(idx)
        ohs.append(oh)
        work = jnp.where(oh, -jnp.inf, work)
    den = es[0] + es[1] + es[2] + es[3]
    ohf = [o.astype(F32) for o in ohs]
    cnt = _dot(jnp.concatenate([o.astype(BF16) for o in ohf], axis=0), tri_ref[...])
    base = run_ref[:, 0:1]
    r8 = lax.broadcasted_iota(I32, (8, tm), 0)
    r128 = lax.broadcasted_iota(I32, (LANES, tm), 0)
    ti8 = jnp.zeros((8, tm), I32)
    rk8 = jnp.zeros((8, tm), I32)
    twt = jnp.zeros((LANES, tm), F32)
    for kk in range(TOP_K):
        rank = jnp.sum(ohf[kk] * (base + cnt[kk * ne:(kk + 1) * ne]), axis=0, keepdims=True)
        base = base + jnp.sum(ohf[kk], axis=1, keepdims=True)
        ti8 = jnp.where(r8 == kk, idxs[kk], ti8)
        rk8 = jnp.where(r8 == kk, rank.astype(I32), rk8)
        twt = jnp.where(r128 == kk, es[kk] / den, twt)
    tile_cnt = base - run_ref[:, 0:1]
    tcnt_ref[0] = jnp.broadcast_to(tile_cnt, run_ref.shape)
    run_ref[...] = run_ref[...] + jnp.floor((tile_cnt + (SEG - 1.0)) * (1.0 / SEG)) * SEG
    ti_ref[...] = ti8
    rk_ref[...] = rk8
    tw_ref[...] = jnp.transpose(twt)
    cnt_ref[...] = run_ref[...]


def merge_and_route(x, mod, gpre, gpost, gffn, ya, yb, yg, yd, wbg, bbg, wbr, wout, wrh, wrl, br,
                    n_tokens, n_lat, seq_len, n_experts):
    d = x.shape[1]
    n_lat_tiles = n_lat // TM
    tiles_per_seq = seq_len // TM

    def mod_row(j):
        return jnp.where(j < n_lat_tiles, j // tiles_per_seq, 4)

    tri = jnp.triu(jnp.ones((TM, TM), F32), 1).astype(BF16)
    tok = lambda w: pl.BlockSpec((TM, w), lambda j: (j, 0))
    lanes = pl.BlockSpec((8, TM), lambda j: (0, j))
    full = lambda shape: pl.BlockSpec(shape, lambda j: tuple(0 for _ in shape))
    return pl.pallas_call(
        functools.partial(_merge_kernel, tiles_per_seq, n_lat_tiles),
        grid=(n_tokens // TM,),
        in_specs=[
            tok(d),
            pl.BlockSpec((1, 6, d), lambda j: (mod_row(j), 0, 0)),
            full((1, d)), full((1, d)), full((1, d)),
            tok(256), tok(256), tok(256), tok(256),
            full(wbg.shape), full(bbg.shape), full(wbr.shape), full(wout.shape),
            full(wrh.shape), full(wrl.shape), full(br.shape), full(tri.shape),
        ],
        out_specs=[tok(d), tok(d // 2), lanes, lanes, tok(LANES), full((n_experts, LANES)),
                   pl.BlockSpec((1, n_experts, LANES), lambda j: (j, 0, 0))],
        out_shape=[
            jax.ShapeDtypeStruct((n_tokens, d), F32),
            jax.ShapeDtypeStruct((n_tokens, d // 2), jnp.uint32),
            jax.ShapeDtypeStruct((8, n_tokens), I32),
            jax.ShapeDtypeStruct((8, n_tokens), I32),
            jax.ShapeDtypeStruct((n_tokens, LANES), F32),
            jax.ShapeDtypeStruct((n_experts, LANES), F32),
            jax.ShapeDtypeStruct((n_tokens // TM, n_experts, LANES), F32),
        ],
        scratch_shapes=[pltpu.VMEM((n_experts, LANES), F32)],
        compiler_params=_cparams(("arbitrary",)),
        name="merge_and_route",
    )(x, mod, gpre, gpost, gffn, ya, yb, yg, yd, wbg, bbg, wbr, wout, wrh, wrl, br, tri)


def _expert_kernel(layer, be_ref, nv_ref, nxt_ref, par_ref, x_ref, b1_ref, b2_ref, w1_hbm, w2_hbm, o_ref,
                   w1f_ref, w2f_ref, w1b_ref, w2b_ref, sem):
    i = pl.program_id(0)
    prev = be_ref[jnp.maximum(i - 1, 0)]
    changed = jnp.logical_or(i == 0, be_ref[i] != prev)
    valid = i < nv_ref[0]
    slot = par_ref[i]

    def fetch(e, s):
        return (pltpu.make_async_copy(w1_hbm.at[layer, e], w1f_ref.at[s], sem.at[0, s]),
                pltpu.make_async_copy(w2_hbm.at[layer, e], w2f_ref.at[s], sem.at[1, s]))

    @pl.when(jnp.logical_and(valid, i == 0))
    def _():
        for cp in fetch(be_ref[0], slot):
            cp.start()

    @pl.when(jnp.logical_and(valid, changed))
    def _():
        for cp in fetch(be_ref[i], slot):
            cp.wait()

        @pl.when(nxt_ref[i] >= 0)
        def _():
            for cp in fetch(nxt_ref[i], 1 - slot):
                cp.start()

        w1b_ref[...] = w1f_ref[slot].astype(BF16)
        w2b_ref[...] = w2f_ref[slot].astype(BF16)

    @pl.when(valid)
    def _():
        w = x_ref[...]
        half = w.shape[1]
        x_lo = lax.bitcast_convert_type(w << 16, F32).astype(BF16)
        x_hi = lax.bitcast_convert_type(w & jnp.uint32(0xFFFF0000), F32).astype(BF16)
        h = _dot(x_lo, w1b_ref[0:half, :]) + _dot(x_hi, w1b_ref[half:, :]) + b1_ref[0]
        de = h.shape[1] // 2
        g = jnp.minimum(h[:, :de], SWIGLU_LIMIT)
        lin = jnp.clip(h[:, de:], -SWIGLU_LIMIT, SWIGLU_LIMIT)
        act = g * jax.nn.sigmoid(SWIGLU_ALPHA * g) * (lin + 1.0)
        y = _dot(act.astype(BF16), w2b_ref[...]) + b2_ref[0]
        bits = lax.bitcast_convert_type(y.astype(BF16).astype(F32), jnp.uint32)
        o_ref[...] = (bits[:, :half] >> 16) | (bits[:, half:] & jnp.uint32(0xFFFF0000))

    @pl.when(i >= nv_ref[0])
    def _():
        o_ref[...] = jnp.zeros_like(o_ref)


def moe_experts(xb, block_e, n_valid, next_e, parity, w1, b1, w2, b2, layer):
    p = xb.shape[0]
    depth, ne, d, de2 = w1.shape
    n_blocks = p // MOE_BM
    grid_spec = pltpu.PrefetchScalarGridSpec(
        num_scalar_prefetch=4,
        grid=(n_blocks,),
        in_specs=[
            pl.BlockSpec((MOE_BM, d // 2), lambda i, be, *_: (i, 0)),
            pl.BlockSpec((None, 1, 1, de2), lambda i, be, *_: (layer, be[i], 0, 0)),
            pl.BlockSpec((None, 1, 1, d), lambda i, be, *_: (layer, be[i], 0, 0)),
            pl.BlockSpec(memory_space=pl.ANY),
            pl.BlockSpec(memory_space=pl.ANY),
        ],
        out_specs=pl.BlockSpec((MOE_BM, d // 2), lambda i, be, *_: (i, 0)),
        scratch_shapes=[pltpu.VMEM((2, d, de2), F32), pltpu.VMEM((2, de2 // 2, d), F32),
                        pltpu.VMEM((d, de2), BF16), pltpu.VMEM((de2 // 2, d), BF16),
                        pltpu.SemaphoreType.DMA((2, 2))],
    )
    return pl.pallas_call(
        functools.partial(_expert_kernel, layer),
        grid_spec=grid_spec,
        out_shape=jax.ShapeDtypeStruct((p, d // 2), jnp.uint32),
        compiler_params=_cparams(("arbitrary",)),
        name="moe_experts",
    )(block_e, n_valid, next_e, parity, xb, b1.reshape(depth, ne, 1, de2), b2.reshape(depth, ne, 1, d), w1, w2)


def _combine_kernel(n_experts, src_ref, len_ref, off_ref, dlt_ref, x_ref, mod_ref, g_ref, tw_ref, ti_ref,
                    rk_ref, yblk_ref, o_ref, gbuf, sem):
    j = pl.program_id(0)
    n = pl.num_programs(0)
    slot = j % 2
    tm = x_ref.shape[0]
    rows = gbuf.shape[1]

    def seg_copy(tile, s, e, c):
        base = tile * n_experts + e
        src = pl.multiple_of(src_ref[base] + c * SEG, SEG)
        dst = pl.multiple_of(off_ref[base] + c * SEG, SEG)
        return pltpu.make_async_copy(yblk_ref.at[pl.ds(src, SEG)], gbuf.at[s, pl.ds(dst, SEG)], sem.at[s])

    def for_each_piece(tile, s, fn):
        for e in range(n_experts):
            pieces = (len_ref[tile * n_experts + e] + (SEG - 1)) // SEG

            def body(c, carry, e=e):
                fn(seg_copy(tile, s, e, c))
                return carry
            lax.fori_loop(0, pieces, body, 0)

    @pl.when(j == 0)
    def _():
        gbuf[...] = jnp.zeros_like(gbuf)
        for_each_piece(j, slot, lambda cp: cp.start())

    @pl.when(j + 1 < n)
    def _():
        for_each_piece(j + 1, 1 - slot, lambda cp: cp.start())

    for_each_piece(j, slot, lambda cp: cp.wait())

    ti = ti_ref[...]
    lp = rk_ref[...]
    for e in range(n_experts):
        lp = lp + jnp.where(ti == e, dlt_ref[j * n_experts + e], 0)
    r128 = lax.broadcasted_iota(I32, (LANES, tm), 0)
    lp128 = jnp.zeros((LANES, tm), F32)
    for kk in range(TOP_K):
        lp128 = jnp.where(r128 == kk, lp[kk:kk + 1].astype(F32), lp128)
    lpn = jnp.transpose(lp128)
    tw = tw_ref[...]
    y_lo = jnp.zeros((tm, gbuf.shape[2]), F32)
    y_hi = jnp.zeros((tm, gbuf.shape[2]), F32)
    for c0 in range(0, rows, COMBINE_KC):
        col = (lax.broadcasted_iota(I32, (tm, COMBINE_KC), 1) + c0).astype(F32)
        q = jnp.zeros((tm, COMBINE_KC), F32)
        for kk in range(TOP_K):
            q = q + jnp.where(col == lpn[:, kk:kk + 1], tw[:, kk:kk + 1], 0.0)
        q = q.astype(BF16)
        w = gbuf[slot, c0:c0 + COMBINE_KC, :]
        y_lo = y_lo + _dot(q, lax.bitcast_convert_type(w << 16, F32).astype(BF16))
        y_hi = y_hi + _dot(q, lax.bitcast_convert_type(w & jnp.uint32(0xFFFF0000), F32).astype(BF16))
    y = jnp.concatenate([y_lo, y_hi], axis=1)
    o_ref[...] = x_ref[...] + mod_ref[0, 5:6, :] * _rms(y, g_ref[...])


def moe_combine(seg_src, seg_len, seg_off, seg_dlt, xn, mod, gain, tw, ti_t, rk_t, yblk,
                n_tokens, n_lat, seq_len, n_experts):
    d = xn.shape[1]
    n_lat_tiles = n_lat // TM
    tiles_per_seq = seq_len // TM
    n_tiles = n_tokens // TM
    rows = TM * TOP_K + n_experts * SEG

    def mod_row(j):
        return jnp.where(j < n_lat_tiles, j // tiles_per_seq, 4)

    grid_spec = pltpu.PrefetchScalarGridSpec(
        num_scalar_prefetch=4,
        grid=(n_tiles,),
        in_specs=[
            pl.BlockSpec((TM, d), lambda j, *_: (j, 0)),
            pl.BlockSpec((1, 6, d), lambda j, *_: (mod_row(j), 0, 0)),
            pl.BlockSpec((1, d), lambda j, *_: (0, 0)),
            pl.BlockSpec((TM, LANES), lambda j, *_: (j, 0)),
            pl.BlockSpec((8, TM), lambda j, *_: (0, j)),
            pl.BlockSpec((8, TM), lambda j, *_: (0, j)),
            pl.BlockSpec(memory_space=pl.ANY),
        ],
        out_specs=pl.BlockSpec((TM, d), lambda j, *_: (j, 0)),
        scratch_shapes=[pltpu.VMEM((2, rows, d // 2), jnp.uint32), pltpu.SemaphoreType.DMA((2,))],
    )
    return pl.pallas_call(
        functools.partial(_combine_kernel, n_experts),
        grid_spec=grid_spec,
        out_shape=jax.ShapeDtypeStruct((n_tokens, d), F32),
        compiler_params=_cparams(("arbitrary",)),
        name="moe_combine",
    )(seg_src, seg_len, seg_off, seg_dlt, xn, mod, gain, tw, ti_t, rk_t, yblk)


def _dest_kernel(n_experts, ps_ref, ti_ref, rk_ref, o_ref):
    ti = ti_ref[...]
    dest = rk_ref[...]
    for e in range(n_experts):
        dest = dest + jnp.where(ti == e, ps_ref[e], 0)
    o_ref[0] = dest


def moe_dest(pstart, ti_t, rk_t):
    n_tokens = ti_t.shape[1]
    lanes = pl.BlockSpec((8, TM), lambda j, ps: (0, j))
    dest = pl.pallas_call(
        functools.partial(_dest_kernel, pstart.shape[0]),
        grid_spec=pltpu.PrefetchScalarGridSpec(
            num_scalar_prefetch=1, grid=(n_tokens // TM,), in_specs=[lanes, lanes],
            out_specs=pl.BlockSpec((1, 8, TM), lambda j, ps: (j, 0, 0))),
        out_shape=jax.ShapeDtypeStruct((n_tokens // TM, 8, TM), I32),
        compiler_params=_cparams(("arbitrary",)),
        name="moe_dest",
    )(pstart, ti_t, rk_t)
    return dest.reshape(-1)


def _dispatch_kernel(dest_ref, hf_ref, xb_in_ref, xb_ref, sem):
    del xb_in_ref
    tm = hf_ref.shape[0]

    def row_copy(t, kk):
        return pltpu.make_async_copy(hf_ref.at[pl.ds(t, 1)], xb_ref.at[pl.ds(dest_ref[kk * tm + t], 1)], sem)

    def start(i, carry):
        t0 = pl.multiple_of(i * DMA_UNROLL, DMA_UNROLL)
        for u in range(DMA_UNROLL):
            for kk in range(TOP_K):
                row_copy(t0 + u, kk).start()
        return carry
    lax.fori_loop(0, tm // DMA_UNROLL, start, 0)

    def wait(i, carry):
        t0 = pl.multiple_of(i * DMA_UNROLL, DMA_UNROLL)
        for u in range(DMA_UNROLL):
            for kk in range(TOP_K):
                row_copy(t0 + u, kk).wait()
        return carry
    lax.fori_loop(0, tm // DMA_UNROLL, wait, 0)


def moe_dispatch(dest_t, hfp, n_rows):
    n_tokens, w = hfp.shape
    xb0 = jnp.zeros((n_rows, w), hfp.dtype)
    return pl.pallas_call(
        _dispatch_kernel,
        grid=(n_tokens // TM,),
        in_specs=[
            pl.BlockSpec((8 * TM,), lambda j: (j,), memory_space=pltpu.SMEM),
            pl.BlockSpec((TM, w), lambda j: (j, 0)),
            pl.BlockSpec(memory_space=pl.ANY),
        ],
        out_specs=pl.BlockSpec(memory_space=pl.ANY),
        out_shape=jax.ShapeDtypeStruct((n_rows, w), hfp.dtype),
        input_output_aliases={2: 0},
        scratch_shapes=[pltpu.SemaphoreType.DMA(())],
        compiler_params=_cparams(("arbitrary",)),
        name="moe_dispatch",
    )(dest_t, hfp, xb0)


def kernel(x, c, ctx, c_ctx, ada_w, ada_b, norm_mix_pre, norm_mix_post, norm_ffn_pre, norm_ffn_post,
           w_in, w_branch_gate, b_branch_gate, w_branch, w_out, pool_w, pool_scale,
           ssm_a_re, ssm_a_im, ssm_log_dt, ssm_b_re, ssm_b_im, ssm_c_re, ssm_c_im, ssm_d, ssm_w_glu,
           gmlp_ln_g, gmlp_ln_b, gmlp_w_s, gmlp_b_s, na_rpb,
           router_w, router_b, expert_w1, expert_b1, expert_w2, expert_b2):
    nb, seq_len, d = x.shape
    ctx_len = ctx.shape[1]
    depth = ada_w.shape[0]
    n_lat = nb * seq_len
    n_all = n_lat + nb * ctx_len
    n_experts = router_w.shape[-1]
    assert nb == 4 and seq_len % TM == 0 and (nb * ctx_len) % TM == 0 and ctx_len % TP == 0

    xs = jnp.concatenate([x.reshape(n_lat, d), ctx.reshape(nb * ctx_len, d)], axis=0)
    cvec = jnp.concatenate([c, c_ctx[None, :], jnp.zeros((3, d), F32)], axis=0)
    mods = ada_modulation(cvec, ada_w, ada_b).reshape(depth, 8, 6, d)
    cos_t, sin_t = rope_tables(seq_len)
    kflat, sall, wst, dec = jax.vmap(s5_tables)(ssm_a_re, ssm_a_im, ssm_log_dt, ssm_b_re, ssm_b_im,
                                                ssm_c_re, ssm_c_im)
    na_tab = jax.vmap(na_bias_table)(na_rpb)

    out = None
    for l in range(depth):
        last = l == depth - 1
        mod = mods[l]
        row = lambda v: v.reshape(1, -1)
        pa, pu, pg, q, k, v = premix(xs, mod, row(norm_mix_pre[l]), w_in[l].astype(BF16), cos_t, sin_t,
                                     n_lat, seq_len)

        n_mix = n_lat if last else n_all
        w_bd = jax.scipy.linalg.block_diag(*[pool_w[l, g] for g in range(POOL_GROUPS)]).astype(BF16)
        ya = pool_mix(pa, w_bd, row(pool_scale[l]), n_mix, n_lat, seq_len, ctx_len)

        bs_full = jnp.repeat(gmlp_b_s[l].T, 256 // GMLP_HEADS, axis=1)
        yg = gmlp_mix(pg, row(gmlp_ln_g[l]), row(gmlp_ln_b[l]), gmlp_w_s[l].astype(BF16), bs_full, n_mix)

        y_g = s5_core(s5_to_groups(pu), kflat, sall, wst, dec, l, ctx_len // SSM_T, seq_len // SSM_T, nb)
        yb = s5_post(s5_from_groups(y_g), pu, row(ssm_d[l]), ssm_w_glu[l].astype(BF16), n_mix)

        yd = na_attention(q, k, v, na_tab, l, nb, seq_len, ctx_len, not last)

        rw = jnp.pad(router_w[l], ((0, 0), (0, LANES - n_experts)))
        rw_hi = rw.astype(BF16)
        rw_lo = (rw - rw_hi.astype(F32)).astype(BF16)
        rb = jnp.pad(router_b[l], (0, LANES - n_experts), constant_values=NEG).reshape(1, LANES)
        n_tok = n_lat if last else n_all
        xn, hfp, ti_t, rk_t, tw, counts, tcnt = merge_and_route(
            xs, mod, row(norm_mix_pre[l]), row(norm_mix_post[l]), row(norm_ffn_pre[l]),
            ya, yb, yg, yd, w_branch_gate[l].astype(BF16), row(b_branch_gate[l]),
            w_branch[l].astype(BF16), w_out[l].astype(BF16), rw_hi, rw_lo, rb, n_tok, n_lat, seq_len, n_experts)

        cnt = counts[:, 0].astype(I32)
        padded = (cnt + MOE_BM - 1) // MOE_BM * MOE_BM
        pend = jnp.cumsum(padded)
        n_blocks = (n_tok * TOP_K + (n_tok // TM) * n_experts * (SEG - 1)) // MOE_BM + 1 + n_experts
        blk_start = jnp.arange(n_blocks, dtype=I32) * MOE_BM
        block_e = jnp.minimum(jnp.sum((pend[None, :] <= blk_start[:, None]).astype(I32), axis=1), n_experts - 1)
        n_valid = (pend[-1] // MOE_BM).astype(I32).reshape(1)
        eidx = jnp.arange(n_experts, dtype=I32)
        nonempty = padded > 0
        later = (eidx[None, :] > eidx[:, None]) & nonempty[None, :]
        nxt_e = jnp.min(jnp.where(later, eidx[None, :], n_experts), axis=1)
        nxt_e = jnp.where(nxt_e >= n_experts, -1, nxt_e)
        par_e = (jnp.cumsum(nonempty.astype(I32)) - 1) % 2
        pick = (block_e[:, None] == eidx[None, :]).astype(I32)
        next_e = jnp.sum(pick * nxt_e[None, :], axis=1)
        parity = jnp.sum(pick * par_e[None, :], axis=1)

        dest_t = moe_dest(pend - padded, ti_t, rk_t)
        xb = moe_dispatch(dest_t, hfp, n_blocks * MOE_BM)
        yblk = moe_experts(xb, block_e, n_valid, next_e, parity, expert_w1, expert_b1, expert_w2, expert_b2, l)
        tc = tcnt[:, :, 0].astype(I32)
        c8 = (tc + SEG - 1) // SEG * SEG
        run = jnp.cumsum(c8, axis=0) - c8
        seg_off = jnp.cumsum(c8, axis=1) - c8
        seg_src = (pend - padded)[None, :] + run
        xs = moe_combine(seg_src.reshape(-1), tc.reshape(-1), seg_off.reshape(-1), (seg_off - run).reshape(-1),
                         xn, mod, row(norm_ffn_post[l]), tw, ti_t, rk_t, yblk, n_tok, n_lat, seq_len, n_experts)
        out = xs
    return out.reshape(nb, seq_len, d)
```
